```python
import math, functools
import jax, jax.numpy as jnp
from jax import lax
import numpy as np

D_MODEL = 1024
BATCH = 8
SEQ = 2048
DEPTH = 2
DEC_BATCH = 32
DEC_SEQ = 1
PAST_LEN = 16384
PAGE_SIZE = 128

N_A_LAYERS = DEPTH // 2
N_B_LAYERS = DEPTH - N_A_LAYERS
A_HEADS = 8
A_DK = 128
A_DV = D_MODEL // A_HEADS
A_FDIM = A_HEADS * A_DK
A_CHUNK = 32
B_HEADS = 8
B_DH = D_MODEL // (2 * B_HEADS)
B_DV = 2 * B_DH
Q_BLOCK = 128
FFN_DIM = 2816
NORM_EPS = 1e-6
NEG_INIT = -1e30

kernel_name = 'yoco_hgrn2_diffattn_decode_step'


def rmsnorm(x, g):
    xf = x.astype(jnp.float32)
    y = xf * lax.rsqrt(jnp.mean(xf * xf, axis=-1, keepdims=True) + NORM_EPS)
    return (y * g.astype(jnp.float32)).astype(x.dtype)


def swiglu(x, w_gu, w_down):
    gate, up = jnp.split(x @ w_gu, 2, axis=-1)
    return (jax.nn.silu(gate) * up) @ w_down


def alibi_slopes(n):
    return jnp.power(2.0, -8.0 * jnp.arange(1, n + 1, dtype=jnp.float32) / n)


def hgrn2_chunked(q, k, v, logf, s0):
    B, T, H, _ = q.shape
    DV = v.shape[-1]
    pad = (-T) % A_CHUNK
    padw = ((0, 0), (0, pad), (0, 0), (0, 0))
    q, k, v, logf = (jnp.pad(a, padw) for a in (q, k, v, logf))
    n = (T + pad) // A_CHUNK

    def to_chunks(a):
        return a.reshape(B, n, A_CHUNK, H, a.shape[-1]).transpose(1, 0, 2, 3, 4)

    causal = jnp.tril(jnp.ones((A_CHUNK, A_CHUNK), dtype=bool))

    def step(S, xs):
        qc, kc, vc, gc = xs
        b = jnp.cumsum(gc, axis=1)
        o_inter = jnp.einsum('bthk,bhkv->bthv', qc * jnp.exp(b), S)
        diff = b[:, :, None] - b[:, None, :]
        decay = jnp.where(causal[None, :, :, None, None], jnp.exp(jnp.minimum(diff, 0.0)), 0.0)
        att = jnp.einsum('bthk,bshk,btshk->bhts', qc, kc, decay)
        o_intra = jnp.einsum('bhts,bshv->bthv', att, vc)
        b_last = b[:, -1]
        S = S * jnp.exp(b_last)[..., None] + jnp.einsum('bshk,bshv->bhkv', kc * jnp.exp(b_last[:, None] - b), vc)
        return S, o_inter + o_intra

    S, o = lax.scan(step, s0, (to_chunks(q), to_chunks(k), to_chunks(v), to_chunks(logf)))
    o = o.transpose(1, 0, 2, 3, 4).reshape(B, n * A_CHUNK, H, DV)[:, :T]
    return o, S


def hgrn2_mixer(h, w_in, w_out, lb, g_norm, s0):
    B, T, _ = h.shape
    proj = h @ w_in
    q, f, i, g = jnp.split(proj, [A_FDIM, 2 * A_FDIM, 2 * A_FDIM + D_MODEL], axis=-1)
    q = jax.nn.silu(q.astype(jnp.float32)).reshape(B, T, A_HEADS, A_DK)
    lbf = lb.astype(jnp.float32)
    forget = lbf + (1.0 - lbf) * jax.nn.sigmoid(f.astype(jnp.float32))
    logf = jnp.log(forget).reshape(B, T, A_HEADS, A_DK)
    k = (1.0 - forget).reshape(B, T, A_HEADS, A_DK)
    v = i.astype(jnp.float32).reshape(B, T, A_HEADS, A_DV)
    o, s_new = hgrn2_chunked(q, k, v, logf, s0.astype(jnp.float32))
    o = rmsnorm(o, g_norm).reshape(B, T, D_MODEL) * jax.nn.silu(g.astype(jnp.float32))
    return o.astype(h.dtype) @ w_out, s_new.astype(s0.dtype)


def shared_kv(x, norm_kv, w_kv):
    B, T, _ = x.shape
    k, v = jnp.split(rmsnorm(x, norm_kv) @ w_kv, [2 * B_HEADS * B_DH], axis=-1)
    return k.reshape(B, T, B_HEADS, 2, B_DH), v.reshape(B, T, B_HEADS, B_DV)


def diff_lambda(lam, layer_idx):
    lam_init = 0.8 - 0.6 * math.exp(-0.3 * layer_idx)
    lam = lam.astype(jnp.float32)
    val = jnp.exp(jnp.sum(lam[0] * lam[1])) - jnp.exp(jnp.sum(lam[2] * lam[3])) + lam_init
    return val, lam_init


def diff_attn_prompt(q, k, v, lam):
    B, T = q.shape[0], q.shape[1]
    nb = T // Q_BLOCK
    slopes = alibi_slopes(B_HEADS)[:, None, None, None]
    kf = k.astype(jnp.float32)
    vf = v.astype(jnp.float32)
    kpos = jnp.arange(T)
    qb = (q.astype(jnp.float32) * (B_DH ** -0.5)).reshape(B, nb, Q_BLOCK, B_HEADS, 2, B_DH).transpose(1, 0, 2, 3, 4, 5)

    def block(args):
        qblk, bi = args
        qpos = bi * Q_BLOCK + jnp.arange(Q_BLOCK)
        s = jnp.einsum('bthjd,bshjd->bhjts', qblk, kf)
        dist = (qpos[:, None] - kpos[None, :]).astype(jnp.float32)
        s = jnp.where(dist >= 0, s - slopes * dist, -jnp.inf)
        p = jax.nn.softmax(s, axis=-1)
        p = p[:, :, 0] - lam * p[:, :, 1]
        return jnp.einsum('bhts,bshv->bthv', p, vf)

    o = lax.map(block, (qb, jnp.arange(nb)))
    return o.transpose(1, 0, 2, 3, 4).reshape(B, T, B_HEADS, B_DV)


def diff_attn_sample(q, k_new, v_new, lam, cache_k, cache_v, page_table):
    DB, T = q.shape[0], q.shape[1]
    n_pages = page_table.shape[1]
    slopes = alibi_slopes(B_HEADS)[:, None, None, None]
    qf = q.astype(jnp.float32) * (B_DH ** -0.5)
    qpos = n_pages * PAGE_SIZE + jnp.arange(T)

    def scores(kblk, kpos):
        s = jnp.einsum('bthjd,bshjd->bhjts', qf, kblk.astype(jnp.float32))
        dist = (qpos[:, None] - kpos[None, :]).astype(jnp.float32)
        return jnp.where(dist >= 0, s - slopes * dist, -jnp.inf)

    def accumulate(carry, s, vblk):
        m, l, acc = carry
        m_new = jnp.maximum(m, jnp.max(s, axis=-1))
        corr = jnp.exp(m - m_new)
        p = jnp.exp(s - m_new[..., None])
        l = l * corr + jnp.sum(p, axis=-1)
        acc = acc * corr[..., None] + jnp.einsum('bhjts,bshv->bhjtv', p, vblk.astype(jnp.float32))
        return (m_new, l, acc)

    def page_step(carry, pidx):
        phys = page_table[:, pidx]
        kpos = pidx * PAGE_SIZE + jnp.arange(PAGE_SIZE)
        return accumulate(carry, scores(cache_k[phys], kpos), cache_v[phys]), None

    init = (jnp.full((DB, B_HEADS, 2, T), NEG_INIT, jnp.float32),
            jnp.zeros((DB, B_HEADS, 2, T), jnp.float32),
            jnp.zeros((DB, B_HEADS, 2, T, B_DV), jnp.float32))
    carry, _ = lax.scan(page_step, init, jnp.arange(n_pages))
    m, l, acc = accumulate(carry, scores(k_new, qpos), v_new)
    o = acc / l[..., None]
    o = o[:, :, 0] - lam * o[:, :, 1]
    return o.transpose(0, 2, 1, 3)


def run_trunk(x, hgrn_state, attend, norm_g, w_ffn_gu, w_ffn_down, a_w_in, a_w_out, a_lb, a_gnorm,
              norm_kv, w_kv, b_w_q, b_w_o, b_lambda, b_subln):
    B, T, _ = x.shape
    lbs = jnp.cumsum(jax.nn.softmax(a_lb.astype(jnp.float32), axis=0), axis=0)
    new_states = []
    k = None
    v = None
    for layer in range(DEPTH):
        g = norm_g[layer]
        x = x + 0.5 * rmsnorm(swiglu(rmsnorm(x, g[0]), w_ffn_gu[layer, 0], w_ffn_down[layer, 0]), g[1])
        h = rmsnorm(x, g[2])
        if layer < N_A_LAYERS:
            mix, s_new = hgrn2_mixer(h, a_w_in[layer], a_w_out[layer], lbs[layer], a_gnorm[layer], hgrn_state[layer])
            new_states.append(s_new)
        else:
            j = layer - N_A_LAYERS
            q = (h @ b_w_q[j]).reshape(B, T, B_HEADS, 2, B_DH)
            lam, lam_init = diff_lambda(b_lambda[j], layer)
            o = attend(q, k, v, lam).astype(h.dtype)
            mix = (rmsnorm(o, b_subln[j]) * (1.0 - lam_init)).reshape(B, T, D_MODEL) @ b_w_o[j]
        x = x + rmsnorm(mix, g[3])
        x = x + 0.5 * rmsnorm(swiglu(rmsnorm(x, g[4]), w_ffn_gu[layer, 1], w_ffn_down[layer, 1]), g[5])
        if layer == N_A_LAYERS - 1:
            k, v = shared_kv(x, norm_kv, w_kv)
    return x, jnp.stack(new_states), k, v


def setup_inputs(seed: int = 0) -> dict:
    key = jax.random.key(seed)
    ks = jax.random.split(key, 20)

    def normal(kk, shape, scale=1.0):
        return jax.random.normal(kk, shape, jnp.float32) * scale

    n_pages = PAST_LEN // PAGE_SIZE
    n_phys = (DEC_BATCH * n_pages * 5) // 4
    page_table = jax.random.permutation(ks[5], n_phys)[:DEC_BATCH * n_pages].reshape(DEC_BATCH, n_pages).astype(jnp.int32)
    return {
        'x_prompt': normal(ks[0], (BATCH, SEQ, D_MODEL)),
        'x_sample': normal(ks[1], (DEC_BATCH, DEC_SEQ, D_MODEL)),
        'cache_k': normal(ks[2], (n_phys, PAGE_SIZE, B_HEADS, 2, B_DH)),
        'cache_v': normal(ks[3], (n_phys, PAGE_SIZE, B_HEADS, B_DV)),
        'state_hgrn': normal(ks[4], (N_A_LAYERS, DEC_BATCH, A_HEADS, A_DK, A_DV), 0.3),
        'page_table': page_table,
        'norm_g': 1.0 + normal(ks[6], (DEPTH, 6, D_MODEL), 0.02),
        'w_ffn_gu': normal(ks[7], (DEPTH, 2, D_MODEL, 2 * FFN_DIM), D_MODEL ** -0.5),
        'w_ffn_down': normal(ks[8], (DEPTH, 2, FFN_DIM, D_MODEL), FFN_DIM ** -0.5),
        'a_w_in': normal(ks[9], (N_A_LAYERS, D_MODEL, 2 * A_FDIM + 2 * D_MODEL), D_MODEL ** -0.5),
        'a_w_out': normal(ks[10], (N_A_LAYERS, D_MODEL, D_MODEL), D_MODEL ** -0.5),
        'a_lb': normal(ks[11], (N_A_LAYERS + 1, A_FDIM), 0.1),
        'a_gnorm': 1.0 + normal(ks[12], (N_A_LAYERS, A_DV), 0.02),
        'norm_kv': 1.0 + normal(ks[13], (D_MODEL,), 0.02),
        'w_kv': normal(ks[14], (D_MODEL, 2 * B_HEADS * B_DH + B_HEADS * B_DV), D_MODEL ** -0.5),
        'b_w_q': normal(ks[15], (N_B_LAYERS, D_MODEL, 2 * B_HEADS * B_DH), D_MODEL ** -0.5),
        'b_w_o': normal(ks[16], (N_B_LAYERS, D_MODEL, D_MODEL), D_MODEL ** -0.5),
        'b_lambda': normal(ks[17], (N_B_LAYERS, 4, B_DH), 0.1),
        'b_subln': 1.0 + normal(ks[18], (N_B_LAYERS, B_DV), 0.02),
    }


def reference(x_prompt, x_sample, cache_k, cache_v, state_hgrn, page_table, norm_g, w_ffn_gu, w_ffn_down,
              a_w_in, a_w_out, a_lb, a_gnorm, norm_kv, w_kv, b_w_q, b_w_o, b_lambda, b_subln):
    weights = (norm_g, w_ffn_gu, w_ffn_down, a_w_in, a_w_out, a_lb, a_gnorm, norm_kv, w_kv,
               b_w_q, b_w_o, b_lambda, b_subln)
    s0_prompt = jnp.zeros((N_A_LAYERS, x_prompt.shape[0], A_HEADS, A_DK, A_DV), state_hgrn.dtype)
    y_prompt, state_hgrn_prompt, k_prompt, v_prompt = run_trunk(x_prompt, s0_prompt, diff_attn_prompt, *weights)
    attend_sample = functools.partial(diff_attn_sample, cache_k=cache_k, cache_v=cache_v, page_table=page_table)
    y_sample, state_hgrn_sample, k_sample, v_sample = run_trunk(x_sample, state_hgrn, attend_sample, *weights)
    return (y_prompt, y_sample, state_hgrn_prompt, k_prompt, v_prompt, state_hgrn_sample, k_sample, v_sample)
```

```python
import functools
import math

import jax
import jax.numpy as jnp
from jax import lax
from jax.experimental import pallas as pl
from jax.experimental.pallas import tpu as pltpu

D_MODEL = 1024
DEPTH = 2
N_A_LAYERS = DEPTH // 2
PAGE_SIZE = 128
A_HEADS = 8
A_DK = 128
A_DV = D_MODEL // A_HEADS
A_FDIM = A_HEADS * A_DK
B_HEADS = 8
B_DH = D_MODEL // (2 * B_HEADS)
B_DV = 2 * B_DH
FFN_DIM = 2816
NORM_EPS = 1e-6
MASK_VALUE = -1e30

LANES = 128
BF16_SUBLANES = 16
VMEM_LIMIT_BYTES = 48 * 1024 * 1024

F32 = jnp.float32
BF16 = jnp.bfloat16


def _rms(x, g):
    ms = jnp.mean(x * x, axis=-1, keepdims=True)
    return x * lax.rsqrt(ms + NORM_EPS) * g


def _silu(x):
    return x * jax.nn.sigmoid(x)


def _dot(a, b):
    return jnp.dot(a, b, preferred_element_type=F32)


def _dot_nt(a, b):
    return lax.dot_general(a, b, (((1,), (1,)), ((), ())), preferred_element_type=F32)


def _dot_tn(a, b):
    return lax.dot_general(a, b, (((0,), (0,)), ((), ())), preferred_element_type=F32)


def _resident(shape):
    nd = len(shape)
    return pl.BlockSpec(shape, lambda *_: (0,) * nd, pipeline_mode=pl.Buffered(1))


def _row_tile(m):
    return min(m, 512)


def _params(*sem):
    return pltpu.CompilerParams(dimension_semantics=sem, vmem_limit_bytes=VMEM_LIMIT_BYTES)


FFN_CHUNK = 256


def _swiglu_kernel(x_ref, gpre_ref, gpost_ref, wgu_ref, wd_ref, o_ref, acc_ref):
    x = x_ref[...]
    xn = _rms(x, gpre_ref[...]).astype(BF16)
    for c in range(FFN_DIM // FFN_CHUNK):
        lo = c * FFN_CHUNK
        gate = _dot(xn, wgu_ref[:, lo:lo + FFN_CHUNK])
        up = _dot(xn, wgu_ref[:, FFN_DIM + lo:FFN_DIM + lo + FFN_CHUNK])
        act = (_silu(gate) * up).astype(BF16)
        down = _dot(act, wd_ref[lo:lo + FFN_CHUNK, :])
        if c == 0:
            acc_ref[...] = down
        else:
            acc_ref[...] += down
    o_ref[...] = x + 0.5 * _rms(acc_ref[...], gpost_ref[...])


def _swiglu_block(x, g_pre, g_post, w_gu, w_down):
    m = x.shape[0]
    tm = _row_tile(m)
    row = pl.BlockSpec((tm, D_MODEL), lambda i: (i, 0))
    vec = _resident((1, D_MODEL))
    return pl.pallas_call(
        _swiglu_kernel,
        grid=(m // tm,),
        in_specs=[row, vec, vec, _resident(w_gu.shape), _resident(w_down.shape)],
        out_specs=row,
        out_shape=jax.ShapeDtypeStruct((m, D_MODEL), F32),
        scratch_shapes=[pltpu.VMEM((tm, D_MODEL), F32)],
        compiler_params=_params("parallel"),
        name="swiglu_block",
    )(x, g_pre.reshape(1, -1), g_post.reshape(1, -1), w_gu, w_down)


def _norm_matmul_kernel(x_ref, g_ref, w_ref, *o_refs):
    xn = _rms(x_ref[...], g_ref[...]).astype(BF16)
    off = 0
    for o_ref in o_refs:
        n = o_ref.shape[-1]
        o_ref[...] = _dot(xn, w_ref[:, off:off + n])
        off += n


def _norm_matmul(x, g, w, splits):
    m = x.shape[0]
    tm = _row_tile(m)
    return pl.pallas_call(
        _norm_matmul_kernel,
        grid=(m // tm,),
        in_specs=[pl.BlockSpec((tm, D_MODEL), lambda i: (i, 0)), _resident((1, D_MODEL)), _resident(w.shape)],
        out_specs=[pl.BlockSpec((tm, n), lambda i: (i, 0)) for n in splits],
        out_shape=[jax.ShapeDtypeStruct((m, n), F32) for n in splits],
        compiler_params=_params("parallel"),
        name="norm_matmul",
    )(x, g.reshape(1, -1), w)


def _matmul_norm_res_kernel(a_ref, w_ref, g_ref, x_ref, o_ref):
    o_ref[...] = x_ref[...] + _rms(_dot(a_ref[...], w_ref[...]), g_ref[...])


def _matmul_norm_res(a, w, g, x):
    m = x.shape[0]
    tm = _row_tile(m)
    row = pl.BlockSpec((tm, D_MODEL), lambda i: (i, 0))
    return pl.pallas_call(
        _matmul_norm_res_kernel,
        grid=(m // tm,),
        in_specs=[row, _resident(w.shape), _resident((1, D_MODEL)), row],
        out_specs=row,
        out_shape=jax.ShapeDtypeStruct((m, D_MODEL), F32),
        compiler_params=_params("parallel"),
        name="matmul_norm_res",
    )(a, w, g.reshape(1, -1), x)


SUB = 16


def _cumsum_rows(x, tri):
    hi = x.astype(BF16)
    r1 = x - hi.astype(F32)
    mid = r1.astype(BF16)
    lo = (r1 - mid.astype(F32)).astype(BF16)
    return _dot(tri, hi) + _dot(tri, mid) + _dot(tri, lo)


def _hgrn_kernel(q_ref, f_ref, i_ref, g_ref, lb_ref, gn_ref, s0_ref, o_ref, s_out_ref,
                 st_ref, b_ref, k_ref, *, chunk, t_valid):
    t_pad = q_ref.shape[1]
    n_chunks = t_pad // chunk
    n_sub = chunk // SUB
    lb = lb_ref[...]
    gn = gn_ref[...]
    ri = lax.broadcasted_iota(jnp.int32, (chunk, chunk), 0)
    ci = lax.broadcasted_iota(jnp.int32, (chunk, chunk), 1)
    tri = jnp.where(ri >= ci, 1.0, 0.0).astype(BF16)
    sub_rows = lax.broadcasted_iota(jnp.int32, (SUB, 1), 0)

    st_ref[...] = s0_ref[0, 0].T

    def body(c, carry):
        base = pl.multiple_of(c * chunk, chunk)
        rows = pl.ds(base, chunk)
        qq = _silu(q_ref[0, rows, :])
        forget = lb + (1.0 - lb) * jax.nn.sigmoid(f_ref[0, rows, :])
        logf = jnp.log(forget)
        kk = 1.0 - forget
        vv = i_ref[0, rows, :]
        if t_valid < t_pad:
            live = (base + lax.broadcasted_iota(jnp.int32, (chunk, 1), 0)) < t_valid
            qq = jnp.where(live, qq, 0.0)
            logf = jnp.where(live, logf, 0.0)
            kk = jnp.where(live, kk, 0.0)
        b = _cumsum_rows(logf, tri)
        b_ref[...] = b
        k_ref[...] = kk
        vb = vv.astype(BF16)

        st = st_ref[...]
        o_inter = _dot_nt((qq * jnp.exp(b)).astype(BF16), st.astype(BF16))

        parts = []
        for i in range(n_sub):
            r0 = i * SUB
            qi = qq[r0:r0 + SUB]
            bi = b[r0:r0 + SUB]
            if i == 0:
                acc = jnp.zeros((SUB, A_DV), F32)
            else:
                ref_row = b_ref[r0 - 1:r0, :]
                qp = (qi * jnp.exp(bi - ref_row)).astype(BF16)
                kp = (kk[:r0] * jnp.exp(ref_row - b[:r0])).astype(BF16)
                att = _dot_nt(qp, kp)
                acc = _dot(att.astype(BF16), vb[:r0])
            for s in range(SUB):
                bs = b_ref[r0 + s:r0 + s + 1, :]
                ks = k_ref[r0 + s:r0 + s + 1, :]
                vs = i_ref[0, pl.ds(base + r0 + s, 1), :]
                w = qi * jnp.exp(jnp.minimum(bi - bs, 0.0)) * ks
                a = jnp.sum(w, axis=-1, keepdims=True)
                acc = acc + jnp.where(sub_rows >= s, a, 0.0) * vs
            parts.append(acc)
        o = o_inter + jnp.concatenate(parts, axis=0)

        b_last = b_ref[chunk - 1:chunk, :]
        kd = (kk * jnp.exp(b_last - b)).astype(BF16)
        st_ref[...] = st * jnp.exp(b_last) + _dot_tn(vb, kd)

        o_ref[0, rows, :] = (_rms(o, gn) * _silu(g_ref[0, rows, :])).astype(BF16)
        return carry

    lax.fori_loop(0, n_chunks, body, 0)
    s_out_ref[0, 0] = st_ref[...].T


def _hgrn_core(q, f, i, g, lb, gnorm, s0, *, chunk, t_valid):
    nb, t_pad, _ = q.shape
    seq = pl.BlockSpec((1, t_pad, A_DK), lambda b, h: (b, 0, h))
    state = pl.BlockSpec((1, 1, A_DK, A_DV), lambda b, h: (b, h, 0, 0))
    kern = functools.partial(_hgrn_kernel, chunk=chunk, t_valid=t_valid)
    return pl.pallas_call(
        kern,
        grid=(nb, A_HEADS),
        in_specs=[seq, seq, seq, seq,
                  pl.BlockSpec((1, A_DK), lambda b, h: (0, h)),
                  pl.BlockSpec((1, A_DV), lambda b, h: (0, 0)),
                  state],
        out_specs=[seq, state],
        out_shape=[jax.ShapeDtypeStruct((nb, t_pad, D_MODEL), BF16),
                   jax.ShapeDtypeStruct((nb, A_HEADS, A_DK, A_DV), F32)],
        scratch_shapes=[pltpu.VMEM((A_DV, A_DK), F32),
                        pltpu.VMEM((chunk, A_DK), F32),
                        pltpu.VMEM((chunk, A_DK), F32)],
        compiler_params=_params("parallel", "parallel"),
        name="hgrn_core",
    )(q, f, i, g, lb.reshape(1, -1), gnorm.reshape(1, -1), s0)


ATTN_TILE = 256


def _attn_prompt_kernel(slopes_ref, lam_ref, q_ref, k_ref, v_ref, sub_ref, o_ref, kb_ref, vb_ref,
                        *, out_scale):
    h = pl.program_id(1)
    qi = pl.program_id(2)
    tq = q_ref.shape[1]
    tk = tq

    @pl.when(qi == 0)
    def _():
        kb_ref[...] = k_ref[0].astype(BF16)
        vb_ref[...] = v_ref[0].astype(BF16)

    slope = slopes_ref[h]
    lam = lam_ref[0]
    q = q_ref[0] * (B_DH ** -0.5)
    lane = lax.broadcasted_iota(jnp.int32, (tq, 2 * B_DH), 1)
    qq = jnp.concatenate([jnp.where(lane < B_DH, q, 0.0), jnp.where(lane >= B_DH, q, 0.0)],
                         axis=0).astype(BF16)
    row = lax.broadcasted_iota(jnp.int32, (2 * tq, tk), 0)
    qpos = qi * tq + jnp.where(row >= tq, row - tq, row)
    col = lax.broadcasted_iota(jnp.int32, (2 * tq, tk), 1)

    def body(j, carry):
        m, l, acc = carry
        kv_rows = pl.ds(pl.multiple_of(j * tk, tk), tk)
        s = _dot_nt(qq, kb_ref[kv_rows, :])
        dist = qpos - (j * tk + col)
        s = jnp.where(dist >= 0, s - slope * dist.astype(F32), MASK_VALUE)
        m_new = jnp.maximum(m, jnp.max(s, axis=-1, keepdims=True))
        alpha = jnp.exp(m - m_new)
        p = jnp.exp(s - m_new)
        l = alpha * l + jnp.sum(p, axis=-1, keepdims=True)
        acc = alpha * acc + _dot(p.astype(BF16), vb_ref[kv_rows, :])
        return m_new, l, acc

    init = (jnp.full((2 * tq, 1), MASK_VALUE, F32), jnp.zeros((2 * tq, 1), F32),
            jnp.zeros((2 * tq, B_DV), F32))
    _, l, acc = lax.fori_loop(0, qi + 1, body, init)
    o = acc / l
    o = o[:tq] - lam * o[tq:]
    o_ref[0] = (_rms(o, sub_ref[...]) * out_scale).astype(BF16)


def _attn_prompt(q, k, v, slopes, lam, subln, out_scale):
    nb, t, _ = q.shape
    tq = min(ATTN_TILE, t)
    smem = pl.BlockSpec(memory_space=pltpu.SMEM)
    kv = pl.BlockSpec((1, t, B_DV), lambda b, h, i: (b, 0, h))
    qo = pl.BlockSpec((1, tq, B_DV), lambda b, h, i: (b, i, h))
    kern = functools.partial(_attn_prompt_kernel, out_scale=out_scale)
    return pl.pallas_call(
        kern,
        grid=(nb, B_HEADS, t // tq),
        in_specs=[smem, smem, qo, kv, kv, pl.BlockSpec((1, B_DV), lambda b, h, i: (0, 0))],
        out_specs=qo,
        out_shape=jax.ShapeDtypeStruct((nb, t, D_MODEL), BF16),
        scratch_shapes=[pltpu.VMEM((t, B_DV), BF16), pltpu.VMEM((t, B_DV), BF16)],
        compiler_params=_params("parallel", "parallel", "arbitrary"),
        name="attn_prompt",
    )(slopes, lam.reshape(1), q, k, v, subln.reshape(1, -1))


N_MAPS = 2 * B_HEADS
PAGE_COLS = PAGE_SIZE * B_HEADS
PAGES_PER_STEP = 4


def _attn_sample_kernel(pt_ref, lam_ref, q_ref, kn_ref, vn_ref, slope_ref, sub_ref, *rest, n_pages, out_scale):
    del pt_ref
    g = PAGES_PER_STEP
    k_refs, v_refs = rest[:g], rest[g:2 * g]
    o_ref, qbd_ref, spread_ref, m_ref, l_ref, acc_ref = rest[2 * g:]
    p = pl.program_id(1)
    n_steps = n_pages // g
    row_head = lax.broadcasted_iota(jnp.int32, (N_MAPS, 1), 0) % B_HEADS

    @pl.when(p == 0)
    def _():
        col = lax.broadcasted_iota(jnp.int32, (N_MAPS, D_MODEL), 1)
        row = lax.broadcasted_iota(jnp.int32, (N_MAPS, D_MODEL), 0)
        mine = (col // B_DV == row % B_HEADS) & ((col // B_DH) % 2 == row // B_HEADS)
        qbd_ref[...] = jnp.where(mine, q_ref[0] * (B_DH ** -0.5), 0.0)
        tok = lax.broadcasted_iota(jnp.int32, (PAGE_SIZE, PAGE_COLS), 0)
        dst = lax.broadcasted_iota(jnp.int32, (PAGE_SIZE, PAGE_COLS), 1)
        spread_ref[...] = jnp.where(dst // B_HEADS == tok, 1.0, 0.0).astype(BF16)
        m_ref[...] = jnp.full(m_ref.shape, MASK_VALUE, F32)
        l_ref[...] = jnp.zeros(l_ref.shape, F32)
        acc_ref[...] = jnp.zeros(acc_ref.shape, F32)

    def accumulate(s, pv_fn):
        m = m_ref[...]
        m_new = jnp.maximum(m, jnp.max(s, axis=-1, keepdims=True))
        alpha = jnp.exp(m - m_new)
        e = jnp.exp(s - m_new)
        l_ref[...] = alpha * l_ref[...] + jnp.sum(e, axis=-1, keepdims=True)
        acc_ref[...] = alpha * acc_ref[...] + pv_fn(e)
        m_ref[...] = m_new

    qbd = qbd_ref[...].astype(BF16)
    s = jnp.concatenate([_dot(qbd, k_ref[0].astype(BF16)) for k_ref in k_refs], axis=1)
    kpos = p * (g * PAGE_SIZE) + lax.broadcasted_iota(jnp.int32, (N_MAPS, g * PAGE_SIZE), 1)
    s = s - slope_ref[...] * (n_pages * PAGE_SIZE - kpos).astype(F32)

    def weighted_values(e):
        own_head = lax.broadcasted_iota(jnp.int32, (N_MAPS, PAGE_COLS), 1) % B_HEADS == row_head
        eb = e.astype(BF16)
        out = None
        for i, v_ref in enumerate(v_refs):
            spread = _dot(eb[:, i * PAGE_SIZE:(i + 1) * PAGE_SIZE], spread_ref[...])
            pv = _dot(jnp.where(own_head, spread, 0.0).astype(BF16), v_ref[0].astype(BF16))
            out = pv if out is None else out + pv
        return out

    accumulate(s, weighted_values)

    @pl.when(p == n_steps - 1)
    def _():
        s_new = jnp.sum(qbd_ref[...] * kn_ref[0], axis=-1, keepdims=True)
        vn = vn_ref[0]
        accumulate(s_new, lambda e: e * jnp.concatenate([vn, vn], axis=0))
        o = acc_ref[...] / l_ref[...]
        o = o[:B_HEADS] - lam_ref[0] * o[B_HEADS:]
        o_ref[0] = _rms(o, sub_ref[...]) * out_scale


def _attn_sample(q, k_new, v_new, cache_k, cache_v, page_table, slopes, lam, subln, out_scale):
    nb = q.shape[0]
    n_pages = page_table.shape[1]
    n_phys = cache_k.shape[0]
    g = PAGES_PER_STEP
    assert n_pages % g == 0
    ck = jnp.transpose(cache_k, (0, 2, 3, 4, 1)).reshape(n_phys, D_MODEL, PAGE_SIZE)
    cv = cache_v.reshape(n_phys, PAGE_COLS, B_DV)
    slope_rows = jnp.tile(slopes, 2).reshape(N_MAPS, 1)
    row = pl.BlockSpec((1, 1, D_MODEL), lambda b, p, pt: (b, 0, 0))
    per_head = pl.BlockSpec((1, B_HEADS, B_DV), lambda b, p, pt: (b, 0, 0))

    def page_spec(shape, i):
        return pl.BlockSpec((1,) + shape, lambda b, p, pt: (pt[b, p * g + i], 0, 0))

    kern = functools.partial(_attn_sample_kernel, n_pages=n_pages, out_scale=out_scale)
    out = pl.pallas_call(
        kern,
        grid_spec=pltpu.PrefetchScalarGridSpec(
            num_scalar_prefetch=1,
            grid=(nb, n_pages // g),
            in_specs=[pl.BlockSpec(memory_space=pltpu.SMEM), row, row, per_head,
                      pl.BlockSpec((N_MAPS, 1), lambda b, p, pt: (0, 0)),
                      pl.BlockSpec((1, B_DV), lambda b, p, pt: (0, 0))]
                     + [page_spec((D_MODEL, PAGE_SIZE), i) for i in range(g)]
                     + [page_spec((PAGE_COLS, B_DV), i) for i in range(g)],
            out_specs=per_head,
            scratch_shapes=[pltpu.VMEM((N_MAPS, D_MODEL), F32),
                            pltpu.VMEM((PAGE_SIZE, PAGE_COLS), BF16),
                            pltpu.VMEM((N_MAPS, 1), F32),
                            pltpu.VMEM((N_MAPS, 1), F32),
                            pltpu.VMEM((N_MAPS, B_DV), F32)]),
        out_shape=jax.ShapeDtypeStruct((nb, B_HEADS, B_DV), F32),
        compiler_params=_params("parallel", "arbitrary"),
        name="attn_sample",
    )(page_table, lam.reshape(1), q.reshape(nb, 1, D_MODEL), k_new.reshape(nb, 1, D_MODEL),
      v_new.reshape(nb, B_HEADS, B_DV), slope_rows, subln.reshape(1, -1), *([ck] * g), *([cv] * g))
    return out.reshape(nb, D_MODEL)


def _diff_lambda(lam, layer_idx):
    lam_init = 0.8 - 0.6 * math.exp(-0.3 * layer_idx)
    lam = lam.astype(F32)
    val = jnp.exp(jnp.sum(lam[0] * lam[1])) - jnp.exp(jnp.sum(lam[2] * lam[3])) + lam_init
    return val, lam_init


def _trunk(x, hgrn_state, attend, w, *, chunk):
    nb, t, _ = x.shape
    m = nb * t
    x = x.reshape(m, D_MODEL)
    lbs = jnp.cumsum(jax.nn.softmax(w["a_lb"].astype(F32), axis=0), axis=0)
    slopes = jnp.power(2.0, -8.0 * jnp.arange(1, B_HEADS + 1, dtype=F32) / B_HEADS)
    t_pad = -(-t // chunk) * chunk
    new_states = []
    k = v = None
    for layer in range(DEPTH):
        g = w["norm_g"][layer]
        x = _swiglu_block(x, g[0], g[1], w["w_ffn_gu"][layer, 0], w["w_ffn_down"][layer, 0])
        if layer < N_A_LAYERS:
            proj = _norm_matmul(x, g[2], w["a_w_in"][layer], (A_FDIM, A_FDIM, D_MODEL, D_MODEL))
            proj = [a.reshape(nb, t, D_MODEL) for a in proj]
            if t_pad != t:
                proj = [jnp.pad(a, ((0, 0), (0, t_pad - t), (0, 0))) for a in proj]
            og, s_new = _hgrn_core(*proj, lbs[layer], w["a_gnorm"][layer], hgrn_state[layer],
                                   chunk=chunk, t_valid=t)
            new_states.append(s_new)
            x = _matmul_norm_res(og[:, :t].reshape(m, D_MODEL), w["a_w_out"][layer], g[3], x)
        else:
            j = layer - N_A_LAYERS
            (q,) = _norm_matmul(x, g[2], w["b_w_q"][j], (D_MODEL,))
            lam, lam_init = _diff_lambda(w["b_lambda"][j], layer)
            oa = attend(q, k, v, slopes, lam, w["b_subln"][j], 1.0 - lam_init)
            x = _matmul_norm_res(oa.astype(BF16), w["b_w_o"][j], g[3], x)
        x = _swiglu_block(x, g[4], g[5], w["w_ffn_gu"][layer, 1], w["w_ffn_down"][layer, 1])
        if layer == N_A_LAYERS - 1:
            k, v = _norm_matmul(x, w["norm_kv"], w["w_kv"], (D_MODEL, D_MODEL))
    return x.reshape(nb, t, D_MODEL), jnp.stack(new_states), k, v


def kernel(x_prompt, x_sample, cache_k, cache_v, state_hgrn, page_table, norm_g, w_ffn_gu, w_ffn_down,
           a_w_in, a_w_out, a_lb, a_gnorm, norm_kv, w_kv, b_w_q, b_w_o, b_lambda, b_subln):
    w = dict(norm_g=norm_g, a_lb=a_lb, a_gnorm=a_gnorm, norm_kv=norm_kv, b_lambda=b_lambda, b_subln=b_subln,
             w_ffn_gu=w_ffn_gu.astype(BF16), w_ffn_down=w_ffn_down.astype(BF16),
             a_w_in=a_w_in.astype(BF16), a_w_out=a_w_out.astype(BF16), w_kv=w_kv.astype(BF16),
             b_w_q=b_w_q.astype(BF16), b_w_o=b_w_o.astype(BF16))
    nb, t, _ = x_prompt.shape
    db, dt, _ = x_sample.shape

    def attend_prompt(q, k, v, slopes, lam, subln, out_scale):
        as_seq = lambda a: a.reshape(nb, t, D_MODEL)
        return _attn_prompt(as_seq(q), as_seq(k), as_seq(v), slopes, lam, subln, out_scale).reshape(nb * t, D_MODEL)

    def attend_sample(q, k, v, slopes, lam, subln, out_scale):
        as_tok = lambda a: a.reshape(db, 1, D_MODEL)
        return _attn_sample(as_tok(q), as_tok(k), as_tok(v), cache_k, cache_v, page_table, slopes, lam, subln,
                            out_scale)

    s0_prompt = jnp.zeros((N_A_LAYERS, nb, A_HEADS, A_DK, A_DV), state_hgrn.dtype)
    y_p, st_p, k_p, v_p = _trunk(x_prompt, s0_prompt, attend_prompt, w, chunk=32)
    y_s, st_s, k_s, v_s = _trunk(x_sample, state_hgrn, attend_sample, w, chunk=BF16_SUBLANES)
    return (y_p, y_s, st_p,
            k_p.reshape(nb, t, B_HEADS, 2, B_DH), v_p.reshape(nb, t, B_HEADS, B_DV),
            st_s,
            k_s.reshape(db, dt, B_HEADS, 2, B_DH), v_s.reshape(db, dt, B_HEADS, B_DV))
```

```python
import functools
import math

import jax
import jax.numpy as jnp
import numpy as np
from jax import lax
from jax.experimental import pallas as pl
from jax.experimental.pallas import tpu as pltpu

D_MODEL = 1024
DEPTH = 2
N_A_LAYERS = DEPTH // 2
PAGE_SIZE = 128
A_HEADS = 8
A_DK = 128
A_DV = D_MODEL // A_HEADS
A_FDIM = A_HEADS * A_DK
B_HEADS = 8
B_DH = D_MODEL // (2 * B_HEADS)
B_DV = 2 * B_DH
FFN_DIM = 2816
NORM_EPS = 1e-6
MASK_VALUE = -1e30

LANES = 128
BF16_SUBLANES = 16
VMEM_LIMIT_BYTES = 48 * 1024 * 1024

F32 = jnp.float32
BF16 = jnp.bfloat16


def _rms(x, g):
    ms = jnp.mean(x * x, axis=-1, keepdims=True)
    return x * lax.rsqrt(ms + NORM_EPS) * g


def _silu(x):
    return x * jax.nn.sigmoid(x)


def _dot(a, b):
    return jnp.dot(a, b, preferred_element_type=F32)


def _dot_nt(a, b):
    return lax.dot_general(a, b, (((1,), (1,)), ((), ())), preferred_element_type=F32)


def _dot_tn(a, b):
    return lax.dot_general(a, b, (((0,), (0,)), ((), ())), preferred_element_type=F32)


def _resident(shape):
    nd = len(shape)
    return pl.BlockSpec(shape, lambda *_: (0,) * nd, pipeline_mode=pl.Buffered(1))


def _row_tile(m):
    return min(m, 512)


def _params(*sem):
    return pltpu.CompilerParams(dimension_semantics=sem, vmem_limit_bytes=VMEM_LIMIT_BYTES)


FFN_CHUNK = 256


def _swiglu_kernel(x_ref, gpre_ref, gpost_ref, wgu_ref, wd_ref, o_ref, acc_ref):
    x = x_ref[...]
    xn = _rms(x, gpre_ref[...]).astype(BF16)
    for c in range(FFN_DIM // FFN_CHUNK):
        lo = c * FFN_CHUNK
        gate = _dot(xn, wgu_ref[:, lo:lo + FFN_CHUNK])
        up = _dot(xn, wgu_ref[:, FFN_DIM + lo:FFN_DIM + lo + FFN_CHUNK])
        act = (_silu(gate) * up).astype(BF16)
        down = _dot(act, wd_ref[lo:lo + FFN_CHUNK, :])
        if c == 0:
            acc_ref[...] = down
        else:
            acc_ref[...] += down
    o_ref[...] = x + 0.5 * _rms(acc_ref[...], gpost_ref[...])


def _swiglu_block(x, g_pre, g_post, w_gu, w_down):
    m = x.shape[0]
    tm = _row_tile(m)
    row = pl.BlockSpec((tm, D_MODEL), lambda i: (i, 0))
    vec = _resident((1, D_MODEL))
    return pl.pallas_call(
        _swiglu_kernel,
        grid=(m // tm,),
        in_specs=[row, vec, vec, _resident(w_gu.shape), _resident(w_down.shape)],
        out_specs=row,
        out_shape=jax.ShapeDtypeStruct((m, D_MODEL), F32),
        scratch_shapes=[pltpu.VMEM((tm, D_MODEL), F32)],
        compiler_params=_params("parallel"),
        name="swiglu_block",
    )(x, g_pre.reshape(1, -1), g_post.reshape(1, -1), w_gu, w_down)


def _norm_matmul_kernel(x_ref, g_ref, w_ref, *o_refs):
    xn = _rms(x_ref[...], g_ref[...]).astype(BF16)
    off = 0
    for o_ref in o_refs:
        n = o_ref.shape[-1]
        o_ref[...] = _dot(xn, w_ref[:, off:off + n])
        off += n


def _norm_matmul(x, g, w, splits):
    m = x.shape[0]
    tm = _row_tile(m)
    return pl.pallas_call(
        _norm_matmul_kernel,
        grid=(m // tm,),
        in_specs=[pl.BlockSpec((tm, D_MODEL), lambda i: (i, 0)), _resident((1, D_MODEL)), _resident(w.shape)],
        out_specs=[pl.BlockSpec((tm, n), lambda i: (i, 0)) for n in splits],
        out_shape=[jax.ShapeDtypeStruct((m, n), F32) for n in splits],
        compiler_params=_params("parallel"),
        name="norm_matmul",
    )(x, g.reshape(1, -1), w)


def _matmul_norm_res_kernel(a_ref, w_ref, g_ref, x_ref, o_ref):
    o_ref[...] = x_ref[...] + _rms(_dot(a_ref[...], w_ref[...]), g_ref[...])


def _matmul_norm_res(a, w, g, x):
    m = x.shape[0]
    tm = _row_tile(m)
    row = pl.BlockSpec((tm, D_MODEL), lambda i: (i, 0))
    return pl.pallas_call(
        _matmul_norm_res_kernel,
        grid=(m // tm,),
        in_specs=[row, _resident(w.shape), _resident((1, D_MODEL)), row],
        out_specs=row,
        out_shape=jax.ShapeDtypeStruct((m, D_MODEL), F32),
        compiler_params=_params("parallel"),
        name="matmul_norm_res",
    )(a, w, g.reshape(1, -1), x)


SUB = 16


def _cumsum_rows(x, tri):
    hi = x.astype(BF16)
    r1 = x - hi.astype(F32)
    mid = r1.astype(BF16)
    lo = (r1 - mid.astype(F32)).astype(BF16)
    return _dot(tri, hi) + _dot(tri, mid) + _dot(tri, lo)


HGRN_TILE = 256


def _hgrn_kernel(q_ref, f_ref, i_ref, g_ref, lb_ref, gn_ref, s0_ref, o_ref, s_out_ref,
                 st_ref, b_ref, k_ref, *, chunk, t_valid):
    tile = q_ref.shape[1]
    n_sub = chunk // SUB
    ti = pl.program_id(1)
    heads = [slice(h * A_DK, (h + 1) * A_DK) for h in range(A_HEADS)]
    lb = lb_ref[...]
    gn = gn_ref[...]
    ri = lax.broadcasted_iota(jnp.int32, (chunk, chunk), 0)
    ci = lax.broadcasted_iota(jnp.int32, (chunk, chunk), 1)
    tri = jnp.where(ri >= ci, 1.0, 0.0).astype(BF16)
    sub_rows = lax.broadcasted_iota(jnp.int32, (SUB, 1), 0)

    @pl.when(ti == 0)
    def _():
        for h in range(A_HEADS):
            st_ref[h] = s0_ref[0, h].T

    def body(c, carry):
        base = pl.multiple_of(c * chunk, chunk)
        rows = pl.ds(base, chunk)
        qq = _silu(q_ref[0, rows, :])
        forget = lb + (1.0 - lb) * jax.nn.sigmoid(f_ref[0, rows, :])
        logf = jnp.log(forget)
        kk = 1.0 - forget
        if t_valid % chunk:
            live = (ti * tile + base + lax.broadcasted_iota(jnp.int32, (chunk, 1), 0)) < t_valid
            qq = jnp.where(live, qq, 0.0)
            logf = jnp.where(live, logf, 0.0)
            kk = jnp.where(live, kk, 0.0)
        b = _cumsum_rows(logf, tri)
        b_ref[...] = b
        k_ref[...] = kk
        vb = i_ref[0, rows, :].astype(BF16)
        b_last = b_ref[chunk - 1:chunk, :]
        qe = (qq * jnp.exp(b)).astype(BF16)
        kd = (kk * jnp.exp(b_last - b)).astype(BF16)
        decay = jnp.exp(b_last)
        gate = _silu(g_ref[0, rows, :])

        intra = []
        for i in range(n_sub):
            r0 = i * SUB
            qi = qq[r0:r0 + SUB]
            bi = b[r0:r0 + SUB]
            if i == 0:
                acc = [jnp.zeros((SUB, A_DV), F32) for _ in heads]
            else:
                ref_row = b_ref[r0 - 1:r0, :]
                qp = (qi * jnp.exp(bi - ref_row)).astype(BF16)
                kp = (kk[:r0] * jnp.exp(ref_row - b[:r0])).astype(BF16)
                acc = [_dot(_dot_nt(qp[:, hs], kp[:, hs]).astype(BF16), vb[:r0, hs]) for hs in heads]
            for s in range(SUB):
                bs = b_ref[r0 + s:r0 + s + 1, :]
                ks = k_ref[r0 + s:r0 + s + 1, :]
                vs = i_ref[0, pl.ds(base + r0 + s, 1), :]
                w = qi * jnp.exp(jnp.minimum(bi - bs, 0.0)) * ks
                for h, hs in enumerate(heads):
                    a = jnp.sum(w[:, hs], axis=-1, keepdims=True)
                    acc[h] = acc[h] + jnp.where(sub_rows >= s, a, 0.0) * vs[:, hs]
            intra.append(acc)

        for h, hs in enumerate(heads):
            st = st_ref[h]
            o = _dot_nt(qe[:, hs], st.astype(BF16)) + jnp.concatenate([acc[h] for acc in intra], axis=0)
            st_ref[h] = st * decay[:, hs] + _dot_tn(vb[:, hs], kd[:, hs])
            o_ref[0, rows, hs] = (_rms(o, gn) * gate[:, hs]).astype(BF16)
        return carry

    lax.fori_loop(0, tile // chunk, body, 0)

    @pl.when(ti == pl.num_programs(1) - 1)
    def _():
        for h in range(A_HEADS):
            s_out_ref[0, h] = st_ref[h].T


def _hgrn_core(q, f, i, g, lb, gnorm, s0, *, chunk, t_valid):
    nb, t_pad, _ = q.shape
    tile = min(HGRN_TILE, t_pad)
    seq = pl.BlockSpec((1, tile, D_MODEL), lambda b, t: (b, t, 0))
    state = pl.BlockSpec((1, A_HEADS, A_DK, A_DV), lambda b, t: (b, 0, 0, 0))
    kern = functools.partial(_hgrn_kernel, chunk=chunk, t_valid=t_valid)
    return pl.pallas_call(
        kern,
        grid=(nb, t_pad // tile),
        in_specs=[seq, seq, seq, seq,
                  pl.BlockSpec((1, A_FDIM), lambda b, t: (0, 0)),
                  pl.BlockSpec((1, A_DV), lambda b, t: (0, 0)),
                  state],
        out_specs=[seq, state],
        out_shape=[jax.ShapeDtypeStruct((nb, t_pad, D_MODEL), BF16),
                   jax.ShapeDtypeStruct((nb, A_HEADS, A_DK, A_DV), F32)],
        scratch_shapes=[pltpu.VMEM((A_HEADS, A_DV, A_DK), F32),
                        pltpu.VMEM((chunk, A_FDIM), F32),
                        pltpu.VMEM((chunk, A_FDIM), F32)],
        compiler_params=_params("parallel", "arbitrary"),
        name="hgrn_core",
    )(q, f, i, g, lb.reshape(1, -1), gnorm.reshape(1, -1), s0)


ATTN_TILE = 512
POS_RADIX = 16
BIAS_ROWS = BF16_SUBLANES


def _attn_prompt_kernel(slopes_ref, lam_ref, q_ref, kt_ref, v_ref, sub_ref, o_ref, ktb_ref, vb_ref,
                        *, out_scale):
    h = pl.program_id(1)
    qi = pl.program_id(2)
    tq = q_ref.shape[1]
    tk = tq
    n_kv = v_ref.shape[1] // tk

    @pl.when(qi == 0)
    def _():
        r = lax.broadcasted_iota(jnp.int32, (BIAS_ROWS, tk), 0)
        c = lax.broadcasted_iota(jnp.int32, (BIAS_ROWS, tk), 1)
        for jt in range(n_kv):
            pos = jt * tk + c
            digits = jnp.where(r == 0, pos // POS_RADIX, jnp.where(r == 1, pos % POS_RADIX,
                               jnp.where(r < 4, 1, 0)))
            for j in range(2):
                ktb_ref[jt, j, :B_DH, :] = kt_ref[0, j * B_DH:(j + 1) * B_DH, jt * tk:(jt + 1) * tk].astype(BF16)
                ktb_ref[jt, j, B_DH:, :] = digits.astype(F32).astype(BF16)
        vb_ref[...] = v_ref[0].astype(BF16)

    slope = slopes_ref[h]
    lam = lam_ref[0]
    q = q_ref[0] * (B_DH ** -0.5)
    qpos = qi * tq + lax.broadcasted_iota(jnp.int32, (tq, BIAS_ROWS), 0)
    lane = lax.broadcasted_iota(jnp.int32, (tq, BIAS_ROWS), 1)
    hi = (qpos // POS_RADIX).astype(F32)
    lo = (qpos % POS_RADIX).astype(F32)
    bias_cols = jnp.where(lane == 0, POS_RADIX * slope, jnp.where(lane == 1, slope,
                          jnp.where(lane == 2, -POS_RADIX * slope * hi, jnp.where(lane == 3, -slope * lo, 0.0))))
    qa = [jnp.concatenate([q[:, j * B_DH:(j + 1) * B_DH], bias_cols], axis=1).astype(BF16) for j in range(2)]
    below_diag = (lax.broadcasted_iota(jnp.int32, (tq, tk), 1) <= lax.broadcasted_iota(jnp.int32, (tq, tk), 0))

    def step(jt, carry, diagonal):
        vt = vb_ref[pl.ds(pl.multiple_of(jt * tk, tk), tk), :]
        out = []
        for j in range(2):
            m, l, acc = carry[j]
            s = _dot(qa[j], ktb_ref[jt, j])
            if diagonal:
                s = jnp.where(below_diag, s, MASK_VALUE)
            m_new = jnp.maximum(m, jnp.max(s, axis=-1, keepdims=True))
            alpha = jnp.exp(m - m_new)
            p = jnp.exp(s - m_new)
            l = alpha * l + jnp.sum(p, axis=-1, keepdims=True)
            acc = alpha * acc + _dot(p.astype(BF16), vt)
            out.append((m_new, l, acc))
        return tuple(out)

    one = (jnp.full((tq, 1), MASK_VALUE, F32), jnp.zeros((tq, 1), F32), jnp.zeros((tq, B_DV), F32))
    carry = lax.fori_loop(0, qi, lambda jt, c: step(jt, c, False), (one, one))
    (_, l1, acc1), (_, l2, acc2) = step(qi, carry, True)
    o = acc1 / l1 - lam * (acc2 / l2)
    o_ref[0] = (_rms(o, sub_ref[...]) * out_scale).astype(BF16)


def _attn_prompt(q, kt, v, slopes, lam, subln, out_scale):
    nb, t, _ = q.shape
    tq = min(ATTN_TILE, t)
    smem = pl.BlockSpec(memory_space=pltpu.SMEM)
    qo = pl.BlockSpec((1, tq, B_DV), lambda b, h, i: (b, i, h))
    kern = functools.partial(_attn_prompt_kernel, out_scale=out_scale)
    return pl.pallas_call(
        kern,
        grid=(nb, B_HEADS, t // tq),
        in_specs=[smem, smem, qo,
                  pl.BlockSpec((1, 2 * B_DH, t), lambda b, h, i: (b, h, 0)),
                  pl.BlockSpec((1, t, B_DV), lambda b, h, i: (b, 0, h)),
                  pl.BlockSpec((1, B_DV), lambda b, h, i: (0, 0))],
        out_specs=qo,
        out_shape=jax.ShapeDtypeStruct((nb, t, D_MODEL), BF16),
        scratch_shapes=[pltpu.VMEM((t // tq, 2, B_DH + BIAS_ROWS, tq), BF16), pltpu.VMEM((t, B_DV), BF16)],
        compiler_params=_params("parallel", "parallel", "arbitrary"),
        name="attn_prompt",
    )(slopes, lam.reshape(1), q, kt, v, subln.reshape(1, -1))


N_MAPS = 2 * B_HEADS
PAGE_COLS = PAGE_SIZE * B_HEADS
PAGES_PER_STEP = 8


def _attn_sample_kernel(pt_ref, lam_ref, q_ref, kn_ref, vn_ref, slope_ref, sub_ref, *rest, n_pages, out_scale):
    del pt_ref
    g = PAGES_PER_STEP
    k_refs, v_refs = rest[:g], rest[g:2 * g]
    o_ref, qbd_ref, spread_ref, m_ref, l_ref, acc_ref = rest[2 * g:]
    p = pl.program_id(1)
    n_steps = n_pages // g
    row_head = lax.broadcasted_iota(jnp.int32, (N_MAPS, 1), 0) % B_HEADS

    @pl.when(p == 0)
    def _():
        col = lax.broadcasted_iota(jnp.int32, (N_MAPS, D_MODEL), 1)
        row = lax.broadcasted_iota(jnp.int32, (N_MAPS, D_MODEL), 0)
        mine = (col // B_DV == row % B_HEADS) & ((col // B_DH) % 2 == row // B_HEADS)
        qbd_ref[...] = jnp.where(mine, q_ref[0] * (B_DH ** -0.5), 0.0)
        tok = lax.broadcasted_iota(jnp.int32, (PAGE_SIZE, PAGE_COLS), 0)
        dst = lax.broadcasted_iota(jnp.int32, (PAGE_SIZE, PAGE_COLS), 1)
        spread_ref[...] = jnp.where(dst // B_HEADS == tok, 1.0, 0.0).astype(BF16)
        m_ref[...] = jnp.full(m_ref.shape, MASK_VALUE, F32)
        l_ref[...] = jnp.zeros(l_ref.shape, F32)
        acc_ref[...] = jnp.zeros(acc_ref.shape, F32)

    def accumulate(s, pv_fn):
        m = m_ref[...]
        m_new = jnp.maximum(m, jnp.max(s, axis=-1, keepdims=True))
        alpha = jnp.exp(m - m_new)
        e = jnp.exp(s - m_new)
        l_ref[...] = alpha * l_ref[...] + jnp.sum(e, axis=-1, keepdims=True)
        acc_ref[...] = alpha * acc_ref[...] + pv_fn(e)
        m_ref[...] = m_new

    qbd = qbd_ref[...].astype(BF16)
    s = jnp.concatenate([_dot(qbd, k_ref[0].astype(BF16)) for k_ref in k_refs], axis=1)
    kpos = p * (g * PAGE_SIZE) + lax.broadcasted_iota(jnp.int32, (N_MAPS, g * PAGE_SIZE), 1)
    s = s - slope_ref[...] * (n_pages * PAGE_SIZE - kpos).astype(F32)

    def weighted_values(e):
        own_head = lax.broadcasted_iota(jnp.int32, (N_MAPS, PAGE_COLS), 1) % B_HEADS == row_head
        eb = e.astype(BF16)
        out = None
        for i, v_ref in enumerate(v_refs):
            spread = _dot(eb[:, i * PAGE_SIZE:(i + 1) * PAGE_SIZE], spread_ref[...])
            pv = _dot(jnp.where(own_head, spread, 0.0).astype(BF16), v_ref[0].astype(BF16))
            out = pv if out is None else out + pv
        return out

    accumulate(s, weighted_values)

    @pl.when(p == n_steps - 1)
    def _():
        s_new = jnp.sum(qbd_ref[...] * kn_ref[0], axis=-1, keepdims=True)
        vn = vn_ref[0]
        accumulate(s_new, lambda e: e * jnp.concatenate([vn, vn], axis=0))
        o = acc_ref[...] / l_ref[...]
        o = o[:B_HEADS] - lam_ref[0] * o[B_HEADS:]
        o_ref[0] = _rms(o, sub_ref[...]) * out_scale


def _attn_sample(q, k_new, v_new, cache_k, cache_v, page_table, slopes, lam, subln, out_scale):
    nb = q.shape[0]
    n_pages = page_table.shape[1]
    n_phys = cache_k.shape[0]
    g = PAGES_PER_STEP
    assert n_pages % g == 0
    ck = jnp.transpose(cache_k, (0, 2, 3, 4, 1)).reshape(n_phys, D_MODEL, PAGE_SIZE)
    cv = cache_v.reshape(n_phys, PAGE_COLS, B_DV)
    slope_rows = jnp.tile(slopes, 2).reshape(N_MAPS, 1)
    row = pl.BlockSpec((1, 1, D_MODEL), lambda b, p, pt: (b, 0, 0))
    per_head = pl.BlockSpec((1, B_HEADS, B_DV), lambda b, p, pt: (b, 0, 0))

    def page_spec(shape, i):
        return pl.BlockSpec((1,) + shape, lambda b, p, pt: (pt[b, p * g + i], 0, 0))

    kern = functools.partial(_attn_sample_kernel, n_pages=n_pages, out_scale=out_scale)
    out = pl.pallas_call(
        kern,
        grid_spec=pltpu.PrefetchScalarGridSpec(
            num_scalar_prefetch=1,
            grid=(nb, n_pages // g),
            in_specs=[pl.BlockSpec(memory_space=pltpu.SMEM), row, row, per_head,
                      pl.BlockSpec((N_MAPS, 1), lambda b, p, pt: (0, 0)),
                      pl.BlockSpec((1, B_DV), lambda b, p, pt: (0, 0))]
                     + [page_spec((D_MODEL, PAGE_SIZE), i) for i in range(g)]
                     + [page_spec((PAGE_COLS, B_DV), i) for i in range(g)],
            out_specs=per_head,
            scratch_shapes=[pltpu.VMEM((N_MAPS, D_MODEL), F32),
                            pltpu.VMEM((PAGE_SIZE, PAGE_COLS), BF16),
                            pltpu.VMEM((N_MAPS, 1), F32),
                            pltpu.VMEM((N_MAPS, 1), F32),
                            pltpu.VMEM((N_MAPS, B_DV), F32)]),
        out_shape=jax.ShapeDtypeStruct((nb, B_HEADS, B_DV), F32),
        compiler_params=_params("parallel", "arbitrary"),
        name="attn_sample",
    )(page_table, lam.reshape(1), q.reshape(nb, 1, D_MODEL), k_new.reshape(nb, 1, D_MODEL),
      v_new.reshape(nb, B_HEADS, B_DV), slope_rows, subln.reshape(1, -1), *([ck] * g), *([cv] * g))
    return out.reshape(nb, D_MODEL)


def _diff_lambda(lam, layer_idx):
    lam_init = 0.8 - 0.6 * math.exp(-0.3 * layer_idx)
    lam = lam.astype(F32)
    val = jnp.exp(jnp.sum(lam[0] * lam[1])) - jnp.exp(jnp.sum(lam[2] * lam[3])) + lam_init
    return val, lam_init


def _norm_kv_kernel(x_ref, g_ref, wkt_ref, wv_ref, kt_ref, v_ref):
    xn = _rms(x_ref[...], g_ref[...]).astype(BF16)
    kt_ref[0] = _dot_nt(wkt_ref[...], xn)
    v_ref[...] = _dot(xn, wv_ref[...])


def _norm_kv_transposed(x, g, w_kt, w_v, nb, t):
    m = x.shape[0]
    tm = _row_tile(t)
    per_seq = t // tm
    return pl.pallas_call(
        _norm_kv_kernel,
        grid=(m // tm,),
        in_specs=[pl.BlockSpec((tm, D_MODEL), lambda i: (i, 0)), _resident((1, D_MODEL)),
                  _resident(w_kt.shape), _resident(w_v.shape)],
        out_specs=[pl.BlockSpec((1, D_MODEL, tm), lambda i: (i // per_seq, 0, i % per_seq)),
                   pl.BlockSpec((tm, D_MODEL), lambda i: (i, 0))],
        out_shape=[jax.ShapeDtypeStruct((nb, D_MODEL, t), F32), jax.ShapeDtypeStruct((m, D_MODEL), F32)],
        compiler_params=_params("parallel"),
        name="norm_kv_transposed",
    )(x, g.reshape(1, -1), w_kt, w_v)


def _alibi_slopes():
    slopes = [2.0 ** (-8.0 * i / B_HEADS) for i in range(1, B_HEADS + 1)]
    assert all(float(np.asarray(s, dtype=BF16)) == s for s in slopes)
    return jnp.asarray(slopes, F32)


def _trunk(x, hgrn_state, attend, w, *, chunk, k_transposed):
    nb, t, _ = x.shape
    m = nb * t
    x = x.reshape(m, D_MODEL)
    lbs = jnp.cumsum(jax.nn.softmax(w["a_lb"].astype(F32), axis=0), axis=0)
    slopes = _alibi_slopes()
    t_pad = -(-t // chunk) * chunk
    new_states = []
    k = v = None
    for layer in range(DEPTH):
        g = w["norm_g"][layer]
        x = _swiglu_block(x, g[0], g[1], w["w_ffn_gu"][layer, 0], w["w_ffn_down"][layer, 0])
        if layer < N_A_LAYERS:
            proj = _norm_matmul(x, g[2], w["a_w_in"][layer], (A_FDIM, A_FDIM, D_MODEL, D_MODEL))
            proj = [a.reshape(nb, t, D_MODEL) for a in proj]
            if t_pad != t:
                proj = [jnp.pad(a, ((0, 0), (0, t_pad - t), (0, 0))) for a in proj]
            og, s_new = _hgrn_core(*proj, lbs[layer], w["a_gnorm"][layer], hgrn_state[layer],
                                   chunk=chunk, t_valid=t)
            new_states.append(s_new)
            x = _matmul_norm_res(og[:, :t].reshape(m, D_MODEL), w["a_w_out"][layer], g[3], x)
        else:
            j = layer - N_A_LAYERS
            (q,) = _norm_matmul(x, g[2], w["b_w_q"][j], (D_MODEL,))
            lam, lam_init = _diff_lambda(w["b_lambda"][j], layer)
            oa = attend(q, k, v, slopes, lam, w["b_subln"][j], 1.0 - lam_init)
            x = _matmul_norm_res(oa.astype(BF16), w["b_w_o"][j], g[3], x)
        x = _swiglu_block(x, g[4], g[5], w["w_ffn_gu"][layer, 1], w["w_ffn_down"][layer, 1])
        if layer == N_A_LAYERS - 1:
            if k_transposed:
                k, v = _norm_kv_transposed(x, w["norm_kv"], w["w_kv"][:, :D_MODEL].T, w["w_kv"][:, D_MODEL:], nb, t)
            else:
                k, v = _norm_matmul(x, w["norm_kv"], w["w_kv"], (D_MODEL, D_MODEL))
    return x.reshape(nb, t, D_MODEL), jnp.stack(new_states), k, v


def kernel(x_prompt, x_sample, cache_k, cache_v, state_hgrn, page_table, norm_g, w_ffn_gu, w_ffn_down,
           a_w_in, a_w_out, a_lb, a_gnorm, norm_kv, w_kv, b_w_q, b_w_o, b_lambda, b_subln):
    w = dict(norm_g=norm_g, a_lb=a_lb, a_gnorm=a_gnorm, norm_kv=norm_kv, b_lambda=b_lambda, b_subln=b_subln,
             w_ffn_gu=w_ffn_gu.astype(BF16), w_ffn_down=w_ffn_down.astype(BF16),
             a_w_in=a_w_in.astype(BF16), a_w_out=a_w_out.astype(BF16), w_kv=w_kv.astype(BF16),
             b_w_q=b_w_q.astype(BF16), b_w_o=b_w_o.astype(BF16))
    nb, t, _ = x_prompt.shape
    db, dt, _ = x_sample.shape

    def attend_prompt(q, kt, v, slopes, lam, subln, out_scale):
        as_seq = lambda a: a.reshape(nb, t, D_MODEL)
        return _attn_prompt(as_seq(q), kt, as_seq(v), slopes, lam, subln, out_scale).reshape(nb * t, D_MODEL)

    def attend_sample(q, k, v, slopes, lam, subln, out_scale):
        as_tok = lambda a: a.reshape(db, 1, D_MODEL)
        return _attn_sample(as_tok(q), as_tok(k), as_tok(v), cache_k, cache_v, page_table, slopes, lam, subln,
                            out_scale)

    s0_prompt = jnp.zeros((N_A_LAYERS, nb, A_HEADS, A_DK, A_DV), state_hgrn.dtype)
    y_p, st_p, kt_p, v_p = _trunk(x_prompt, s0_prompt, attend_prompt, w, chunk=32, k_transposed=True)
    y_s, st_s, k_s, v_s = _trunk(x_sample, state_hgrn, attend_sample, w, chunk=BF16_SUBLANES, k_transposed=False)
    k_p = jnp.transpose(kt_p.reshape(nb, B_HEADS, 2, B_DH, t), (0, 4, 1, 2, 3))
    return (y_p, y_s, st_p,
            k_p, v_p.reshape(nb, t, B_HEADS, B_DV),
            st_s,
            k_s.reshape(db, dt, B_HEADS, 2, B_DH), v_s.reshape(db, dt, B_HEADS, B_DV))
```

```python
import functools
import math

import jax
import jax.numpy as jnp
import numpy as np
from jax import lax
from jax.experimental import pallas as pl
from jax.experimental.pallas import tpu as pltpu

D_MODEL = 1024
DEPTH = 2
N_A_LAYERS = DEPTH // 2
PAGE_SIZE = 128
A_HEADS = 8
A_DK = 128
A_DV = D_MODEL // A_HEADS
A_FDIM = A_HEADS * A_DK
B_HEADS = 8
B_DH = D_MODEL // (2 * B_HEADS)
B_DV = 2 * B_DH
FFN_DIM = 2816
NORM_EPS = 1e-6
MASK_VALUE = -1e30

LANES = 128
BF16_SUBLANES = 16
VMEM_LIMIT_BYTES = 48 * 1024 * 1024

F32 = jnp.float32
BF16 = jnp.bfloat16


def _rms(x, g):
    ms = jnp.mean(x * x, axis=-1, keepdims=True)
    return x * lax.rsqrt(ms + NORM_EPS) * g


def _silu(x):
    return x * jax.nn.sigmoid(x)


def _dot(a, b):
    return jnp.dot(a, b, preferred_element_type=F32)


def _dot_nt(a, b):
    return lax.dot_general(a, b, (((1,), (1,)), ((), ())), preferred_element_type=F32)


def _dot_tn(a, b):
    return lax.dot_general(a, b, (((0,), (0,)), ((), ())), preferred_element_type=F32)


def _resident(shape):
    nd = len(shape)
    return pl.BlockSpec(shape, lambda *_: (0,) * nd, pipeline_mode=pl.Buffered(1))


def _row_tile(m):
    return min(m, 512)


def _params(*sem):
    return pltpu.CompilerParams(dimension_semantics=sem, vmem_limit_bytes=VMEM_LIMIT_BYTES)


FFN_CHUNK = 256


def _swiglu_kernel(x_ref, gpre_ref, gpost_ref, wgu_ref, wd_ref, o_ref, acc_ref):
    x = x_ref[...]
    xn = _rms(x, gpre_ref[...]).astype(BF16)
    for c in range(FFN_DIM // FFN_CHUNK):
        lo = c * FFN_CHUNK
        gate = _dot(xn, wgu_ref[:, lo:lo + FFN_CHUNK])
        up = _dot(xn, wgu_ref[:, FFN_DIM + lo:FFN_DIM + lo + FFN_CHUNK])
        act = (_silu(gate) * up).astype(BF16)
        down = _dot(act, wd_ref[lo:lo + FFN_CHUNK, :])
        if c == 0:
            acc_ref[...] = down
        else:
            acc_ref[...] += down
    o_ref[...] = x + 0.5 * _rms(acc_ref[...], gpost_ref[...])


def _swiglu_block(x, g_pre, g_post, w_gu, w_down):
    m = x.shape[0]
    tm = _row_tile(m)
    row = pl.BlockSpec((tm, D_MODEL), lambda i: (i, 0))
    vec = _resident((1, D_MODEL))
    return pl.pallas_call(
        _swiglu_kernel,
        grid=(m // tm,),
        in_specs=[row, vec, vec, _resident(w_gu.shape), _resident(w_down.shape)],
        out_specs=row,
        out_shape=jax.ShapeDtypeStruct((m, D_MODEL), F32),
        scratch_shapes=[pltpu.VMEM((tm, D_MODEL), F32)],
        compiler_params=_params("parallel"),
        name="swiglu_block",
    )(x, g_pre.reshape(1, -1), g_post.reshape(1, -1), w_gu, w_down)


def _norm_matmul_kernel(x_ref, g_ref, w_ref, *o_refs):
    xn = _rms(x_ref[...], g_ref[...]).astype(BF16)
    off = 0
    for o_ref in o_refs:
        n = o_ref.shape[-1]
        o_ref[...] = _dot(xn, w_ref[:, off:off + n])
        off += n


def _norm_matmul(x, g, w, splits):
    m = x.shape[0]
    tm = _row_tile(m)
    return pl.pallas_call(
        _norm_matmul_kernel,
        grid=(m // tm,),
        in_specs=[pl.BlockSpec((tm, D_MODEL), lambda i: (i, 0)), _resident((1, D_MODEL)), _resident(w.shape)],
        out_specs=[pl.BlockSpec((tm, n), lambda i: (i, 0)) for n in splits],
        out_shape=[jax.ShapeDtypeStruct((m, n), F32) for n in splits],
        compiler_params=_params("parallel"),
        name="norm_matmul",
    )(x, g.reshape(1, -1), w)


def _matmul_norm_res_kernel(a_ref, w_ref, g_ref, x_ref, o_ref):
    o_ref[...] = x_ref[...] + _rms(_dot(a_ref[...], w_ref[...]), g_ref[...])


def _matmul_norm_res(a, w, g, x):
    m = x.shape[0]
    tm = _row_tile(m)
    row = pl.BlockSpec((tm, D_MODEL), lambda i: (i, 0))
    return pl.pallas_call(
        _matmul_norm_res_kernel,
        grid=(m // tm,),
        in_specs=[row, _resident(w.shape), _resident((1, D_MODEL)), row],
        out_specs=row,
        out_shape=jax.ShapeDtypeStruct((m, D_MODEL), F32),
        compiler_params=_params("parallel"),
        name="matmul_norm_res",
    )(a, w, g.reshape(1, -1), x)


SUB = 16


def _cumsum_rows(x, tri):
    hi = x.astype(BF16)
    r1 = x - hi.astype(F32)
    mid = r1.astype(BF16)
    lo = (r1 - mid.astype(F32)).astype(BF16)
    return _dot(tri, hi) + _dot(tri, mid) + _dot(tri, lo)


HGRN_TILE = 256


def _hgrn_kernel(q_ref, f_ref, i_ref, g_ref, lb_ref, gn_ref, s0_ref, o_ref, s_out_ref,
                 st_ref, b_ref, k_ref, *, chunk, t_valid):
    tile = q_ref.shape[1]
    n_sub = chunk // SUB
    ti = pl.program_id(1)
    heads = [slice(h * A_DK, (h + 1) * A_DK) for h in range(A_HEADS)]
    lb = lb_ref[...]
    gn = gn_ref[...]
    ri = lax.broadcasted_iota(jnp.int32, (chunk, chunk), 0)
    ci = lax.broadcasted_iota(jnp.int32, (chunk, chunk), 1)
    tri = jnp.where(ri >= ci, 1.0, 0.0).astype(BF16)
    half = SUB // 2
    sub_col = lax.broadcasted_iota(jnp.int32, (half, SUB), 1)
    causal_sub = (lax.broadcasted_iota(jnp.int32, (SUB, SUB), 0) >= lax.broadcasted_iota(jnp.int32, (SUB, SUB), 1))

    @pl.when(ti == 0)
    def _():
        for h in range(A_HEADS):
            st_ref[h] = s0_ref[0, h].T

    def body(c, carry):
        base = pl.multiple_of(c * chunk, chunk)
        rows = pl.ds(base, chunk)
        qq = _silu(q_ref[0, rows, :])
        forget = lb + (1.0 - lb) * jax.nn.sigmoid(f_ref[0, rows, :])
        logf = jnp.log(forget)
        kk = 1.0 - forget
        if t_valid % chunk:
            live = (ti * tile + base + lax.broadcasted_iota(jnp.int32, (chunk, 1), 0)) < t_valid
            qq = jnp.where(live, qq, 0.0)
            logf = jnp.where(live, logf, 0.0)
            kk = jnp.where(live, kk, 0.0)
        b = _cumsum_rows(logf, tri)
        b_ref[...] = b
        k_ref[...] = kk
        vb = i_ref[0, rows, :].astype(BF16)
        b_last = b_ref[chunk - 1:chunk, :]
        qe = (qq * jnp.exp(b)).astype(BF16)
        kd = (kk * jnp.exp(b_last - b)).astype(BF16)
        decay = jnp.exp(b_last)
        gate = _silu(g_ref[0, rows, :])

        intra = []
        for i in range(n_sub):
            r0 = i * SUB
            qi = qq[r0:r0 + SUB]
            bi = b[r0:r0 + SUB]
            if i == 0:
                acc = [jnp.zeros((SUB, A_DV), F32) for _ in heads]
            else:
                ref_row = b_ref[r0 - 1:r0, :]
                qp = (qi * jnp.exp(bi - ref_row)).astype(BF16)
                kp = (kk[:r0] * jnp.exp(ref_row - b[:r0])).astype(BF16)
                acc = [_dot(_dot_nt(qp[:, hs], kp[:, hs]).astype(BF16), vb[:r0, hs]) for hs in heads]
            top = [jnp.zeros((half, SUB), F32) for _ in heads]
            bot = [jnp.zeros((half, SUB), F32) for _ in heads]
            for s in range(SUB):
                bs = b_ref[r0 + s:r0 + s + 1, :]
                ks = k_ref[r0 + s:r0 + s + 1, :]
                first = 0 if s < half else half
                w = qi[first:] * jnp.exp(jnp.minimum(bi[first:] - bs, 0.0)) * ks
                for h, hs in enumerate(heads):
                    a = jnp.sum(w[:, hs], axis=-1, keepdims=True)
                    if s < half:
                        top[h] = jnp.where(sub_col == s, a[:half], top[h])
                    bot[h] = jnp.where(sub_col == s, a[-half:], bot[h])
            for h, hs in enumerate(heads):
                att = jnp.where(causal_sub, jnp.concatenate([top[h], bot[h]], axis=0), 0.0)
                acc[h] = acc[h] + _dot(att.astype(BF16), vb[r0:r0 + SUB, hs])
            intra.append(acc)

        for h, hs in enumerate(heads):
            st = st_ref[h]
            o = _dot_nt(qe[:, hs], st.astype(BF16)) + jnp.concatenate([acc[h] for acc in intra], axis=0)
            st_ref[h] = st * decay[:, hs] + _dot_tn(vb[:, hs], kd[:, hs])
            o_ref[0, rows, hs] = (_rms(o, gn) * gate[:, hs]).astype(BF16)
        return carry

    lax.fori_loop(0, tile // chunk, body, 0)

    @pl.when(ti == pl.num_programs(1) - 1)
    def _():
        for h in range(A_HEADS):
            s_out_ref[0, h] = st_ref[h].T


def _hgrn_core(q, f, i, g, lb, gnorm, s0, *, chunk, t_valid):
    nb, t_pad, _ = q.shape
    tile = min(HGRN_TILE, t_pad)
    seq = pl.BlockSpec((1, tile, D_MODEL), lambda b, t: (b, t, 0))
    state = pl.BlockSpec((1, A_HEADS, A_DK, A_DV), lambda b, t: (b, 0, 0, 0))
    kern = functools.partial(_hgrn_kernel, chunk=chunk, t_valid=t_valid)
    return pl.pallas_call(
        kern,
        grid=(nb, t_pad // tile),
        in_specs=[seq, seq, seq, seq,
                  pl.BlockSpec((1, A_FDIM), lambda b, t: (0, 0)),
                  pl.BlockSpec((1, A_DV), lambda b, t: (0, 0)),
                  state],
        out_specs=[seq, state],
        out_shape=[jax.ShapeDtypeStruct((nb, t_pad, D_MODEL), BF16),
                   jax.ShapeDtypeStruct((nb, A_HEADS, A_DK, A_DV), F32)],
        scratch_shapes=[pltpu.VMEM((A_HEADS, A_DV, A_DK), F32),
                        pltpu.VMEM((chunk, A_FDIM), F32),
                        pltpu.VMEM((chunk, A_FDIM), F32)],
        compiler_params=_params("parallel", "arbitrary"),
        name="hgrn_core",
    )(q, f, i, g, lb.reshape(1, -1), gnorm.reshape(1, -1), s0)


ATTN_TILE = 512
POS_RADIX = 16


def _attn_prompt_kernel(slopes_ref, lam_ref, qt_ref, k_ref, vt_ref, sub_ref, o_ref, ka_ref, vtb_ref,
                        sa_ref, sb_ref, pa_ref, pb_ref, *, out_scale):
    h = pl.program_id(1)
    qi = pl.program_id(2)
    tq = qt_ref.shape[2]
    tk = tq
    n_kv = k_ref.shape[1] // tk

    @pl.when(qi == 0)
    def _():
        lane = lax.broadcasted_iota(jnp.int32, (tk, 2 * B_DH), 1)
        row = lax.broadcasted_iota(jnp.int32, (tk, 2 * B_DH), 0)
        for jt in range(n_kv):
            kpos = jt * tk + row
            digits = jnp.where(lane == B_DH, kpos // POS_RADIX, jnp.where(lane == B_DH + 1, kpos % POS_RADIX,
                               jnp.where(lane < B_DH + 4, 1, 0))).astype(F32)
            kblk = k_ref[0, jt * tk:(jt + 1) * tk, :]
            ka_ref[jt, 0] = jnp.where(lane < B_DH, kblk, digits).astype(BF16)
            ka_ref[jt, 1] = jnp.where(lane < B_DH, pltpu.roll(kblk, B_DH, axis=1), digits).astype(BF16)
            vtb_ref[jt] = vt_ref[0, :, jt * tk:(jt + 1) * tk].astype(BF16)

    slope = slopes_ref[h]
    lam = lam_ref[0]
    qt = qt_ref[0] * (B_DH ** -0.5)
    qpos = qi * tq + lax.broadcasted_iota(jnp.int32, (B_DH, tq), 1)
    r = lax.broadcasted_iota(jnp.int32, (B_DH, tq), 0)
    hi = (qpos // POS_RADIX).astype(F32)
    lo = (qpos % POS_RADIX).astype(F32)
    bias_rows = jnp.where(r == 0, POS_RADIX * slope, jnp.where(r == 1, slope,
                          jnp.where(r == 2, -POS_RADIX * slope * hi, jnp.where(r == 3, -slope * lo, 0.0))))
    qa = [jnp.concatenate([qt[j * B_DH:(j + 1) * B_DH], bias_rows], axis=0).astype(BF16) for j in range(2)]
    causal = (lax.broadcasted_iota(jnp.int32, (tk, tq), 0) <= lax.broadcasted_iota(jnp.int32, (tk, tq), 1))

    s_bufs, p_bufs = (sa_ref, sb_ref), (pa_ref, pb_ref)

    def put_scores(buf, tile, diagonal):
        for j in range(2):
            s = _dot(ka_ref[tile, j], qa[j])
            buf[j] = jnp.where(causal, s, MASK_VALUE) if diagonal else s

    def pipeline_step(slot, stats, accs, next_tile, prev_tile):
        if next_tile is not None:
            put_scores(s_bufs[1 - slot], next_tile, False)
        new_stats, new_accs = [], []
        for j in range(2):
            m, l = stats[j]
            pv = _dot(vtb_ref[prev_tile], p_bufs[1 - slot][j])
            s = s_bufs[slot][j]
            m_new = jnp.maximum(m, jnp.max(s, axis=0, keepdims=True))
            alpha = jnp.exp(m - m_new)
            p = jnp.exp(s - m_new)
            p_bufs[slot][j] = p.astype(BF16)
            new_stats.append((m_new, alpha * l + jnp.sum(p, axis=0, keepdims=True)))
            new_accs.append(alpha * (accs[j] + pv))
        return tuple(new_stats), tuple(new_accs)

    def final_step(slot, stats, accs, prev_tile):
        stats, accs = pipeline_step(slot, stats, accs, None, prev_tile)
        last_tile = jnp.maximum(qi - 1, 0)
        o1, o2 = [(accs[j] + _dot(vtb_ref[last_tile], p_bufs[slot][j])) / stats[j][1] for j in range(2)]
        ot = o1 - lam * o2
        ms = jnp.mean(ot * ot, axis=0, keepdims=True)
        ot = ot * lax.rsqrt(ms + NORM_EPS) * (sub_ref[...] * out_scale)
        o_ref[0] = ot.T.astype(BF16)

    put_scores(sa_ref, qi, True)
    pb_ref[...] = jnp.zeros(pb_ref.shape, BF16)

    def pair(i, carry):
        k = 2 * i
        stats, accs = pipeline_step(0, *carry, next_tile=k, prev_tile=jnp.maximum(k - 2, 0))
        return pipeline_step(1, stats, accs, next_tile=k + 1, prev_tile=jnp.where(k == 0, qi, k - 1))

    one = (jnp.full((1, tq), MASK_VALUE, F32), jnp.zeros((1, tq), F32))
    zero = jnp.zeros((B_DV, tq), F32)
    stats, accs = lax.fori_loop(0, qi // 2, pair, ((one, one), (zero, zero)))

    @pl.when(qi % 2 == 0)
    def _():
        final_step(0, stats, accs, jnp.maximum(qi - 2, 0))

    @pl.when(qi % 2 == 1)
    def _():
        k = qi - 1
        st, ac = pipeline_step(0, stats, accs, next_tile=k, prev_tile=jnp.maximum(k - 2, 0))
        final_step(1, st, ac, jnp.where(k == 0, qi, k - 1))


def _attn_prompt(qt, k, vt, slopes, lam, subln, out_scale):
    nb, t, _ = k.shape
    tq = min(ATTN_TILE, t)
    smem = pl.BlockSpec(memory_space=pltpu.SMEM)
    kern = functools.partial(_attn_prompt_kernel, out_scale=out_scale)
    return pl.pallas_call(
        kern,
        grid=(nb, B_HEADS, t // tq),
        in_specs=[smem, smem,
                  pl.BlockSpec((1, 2 * B_DH, tq), lambda b, h, i: (b, h, i)),
                  pl.BlockSpec((1, t, 2 * B_DH), lambda b, h, i: (b, 0, h)),
                  pl.BlockSpec((1, B_DV, t), lambda b, h, i: (b, h, 0)),
                  pl.BlockSpec((B_DV, 1), lambda b, h, i: (0, 0))],
        out_specs=pl.BlockSpec((1, tq, B_DV), lambda b, h, i: (b, i, h)),
        out_shape=jax.ShapeDtypeStruct((nb, t, D_MODEL), BF16),
        scratch_shapes=[pltpu.VMEM((t // tq, 2, tq, 2 * B_DH), BF16), pltpu.VMEM((t // tq, B_DV, tq), BF16),
                        pltpu.VMEM((2, tq, tq), F32), pltpu.VMEM((2, tq, tq), F32),
                        pltpu.VMEM((2, tq, tq), BF16), pltpu.VMEM((2, tq, tq), BF16)],
        compiler_params=_params("parallel", "parallel", "arbitrary"),
        name="attn_prompt",
    )(slopes, lam.reshape(1), qt, k, vt, subln.reshape(-1, 1))


N_MAPS = 2 * B_HEADS
PAGE_COLS = PAGE_SIZE * B_HEADS
PAGES_PER_STEP = 16


def _attn_sample_kernel(pt_ref, lam_ref, q_ref, kn_ref, vn_ref, slope_ref, sub_ref, *rest, n_pages, out_scale):
    del pt_ref
    g = PAGES_PER_STEP
    k_refs, v_refs = rest[:g], rest[g:2 * g]
    o_ref, qbd_ref, spread_ref, m_ref, l_ref, acc_ref = rest[2 * g:]
    p = pl.program_id(1)
    n_steps = n_pages // g
    row_head = lax.broadcasted_iota(jnp.int32, (N_MAPS, 1), 0) % B_HEADS

    @pl.when(p == 0)
    def _():
        col = lax.broadcasted_iota(jnp.int32, (N_MAPS, D_MODEL), 1)
        row = lax.broadcasted_iota(jnp.int32, (N_MAPS, D_MODEL), 0)
        mine = (col // B_DV == row % B_HEADS) & ((col // B_DH) % 2 == row // B_HEADS)
        qbd_ref[...] = jnp.where(mine, q_ref[0] * (B_DH ** -0.5), 0.0)
        tok = lax.broadcasted_iota(jnp.int32, (PAGE_SIZE, PAGE_COLS), 0)
        dst = lax.broadcasted_iota(jnp.int32, (PAGE_SIZE, PAGE_COLS), 1)
        spread_ref[...] = jnp.where(dst // B_HEADS == tok, 1.0, 0.0).astype(BF16)
        m_ref[...] = jnp.full(m_ref.shape, MASK_VALUE, F32)
        l_ref[...] = jnp.zeros(l_ref.shape, F32)
        acc_ref[...] = jnp.zeros(acc_ref.shape, F32)

    def accumulate(s, pv_fn):
        m = m_ref[...]
        m_new = jnp.maximum(m, jnp.max(s, axis=-1, keepdims=True))
        alpha = jnp.exp(m - m_new)
        e = jnp.exp(s - m_new)
        l_ref[...] = alpha * l_ref[...] + jnp.sum(e, axis=-1, keepdims=True)
        acc_ref[...] = alpha * acc_ref[...] + pv_fn(e)
        m_ref[...] = m_new

    qbd = qbd_ref[...].astype(BF16)
    s = jnp.concatenate([_dot(qbd, k_ref[0].astype(BF16)) for k_ref in k_refs], axis=1)
    kpos = p * (g * PAGE_SIZE) + lax.broadcasted_iota(jnp.int32, (N_MAPS, g * PAGE_SIZE), 1)
    s = s - slope_ref[...] * (n_pages * PAGE_SIZE - kpos).astype(F32)

    def weighted_values(e):
        own_head = lax.broadcasted_iota(jnp.int32, (N_MAPS, PAGE_COLS), 1) % B_HEADS == row_head
        eb = e.astype(BF16)
        out = None
        for i, v_ref in enumerate(v_refs):
            spread = _dot(eb[:, i * PAGE_SIZE:(i + 1) * PAGE_SIZE], spread_ref[...])
            pv = _dot(jnp.where(own_head, spread, 0.0).astype(BF16), v_ref[0].astype(BF16))
            out = pv if out is None else out + pv
        return out

    accumulate(s, weighted_values)

    @pl.when(p == n_steps - 1)
    def _():
        s_new = jnp.sum(qbd_ref[...] * kn_ref[0], axis=-1, keepdims=True)
        vn = vn_ref[0]
        accumulate(s_new, lambda e: e * jnp.concatenate([vn, vn], axis=0))
        o = acc_ref[...] / l_ref[...]
        o = o[:B_HEADS] - lam_ref[0] * o[B_HEADS:]
        o_ref[0] = _rms(o, sub_ref[...]) * out_scale


def _attn_sample(q, k_new, v_new, cache_k, cache_v, page_table, slopes, lam, subln, out_scale):
    nb = q.shape[0]
    n_pages = page_table.shape[1]
    n_phys = cache_k.shape[0]
    g = PAGES_PER_STEP
    assert n_pages % g == 0
    ck = jnp.transpose(cache_k, (0, 2, 3, 4, 1)).reshape(n_phys, D_MODEL, PAGE_SIZE)
    cv = cache_v.reshape(n_phys, PAGE_COLS, B_DV)
    slope_rows = jnp.tile(slopes, 2).reshape(N_MAPS, 1)
    row = pl.BlockSpec((1, 1, D_MODEL), lambda b, p, pt: (b, 0, 0))
    per_head = pl.BlockSpec((1, B_HEADS, B_DV), lambda b, p, pt: (b, 0, 0))

    def page_spec(shape, i):
        return pl.BlockSpec((1,) + shape, lambda b, p, pt: (pt[b, p * g + i], 0, 0))

    kern = functools.partial(_attn_sample_kernel, n_pages=n_pages, out_scale=out_scale)
    out = pl.pallas_call(
        kern,
        grid_spec=pltpu.PrefetchScalarGridSpec(
            num_scalar_prefetch=1,
            grid=(nb, n_pages // g),
            in_specs=[pl.BlockSpec(memory_space=pltpu.SMEM), row, row, per_head,
                      pl.BlockSpec((N_MAPS, 1), lambda b, p, pt: (0, 0)),
                      pl.BlockSpec((1, B_DV), lambda b, p, pt: (0, 0))]
                     + [page_spec((D_MODEL, PAGE_SIZE), i) for i in range(g)]
                     + [page_spec((PAGE_COLS, B_DV), i) for i in range(g)],
            out_specs=per_head,
            scratch_shapes=[pltpu.VMEM((N_MAPS, D_MODEL), F32),
                            pltpu.VMEM((PAGE_SIZE, PAGE_COLS), BF16),
                            pltpu.VMEM((N_MAPS, 1), F32),
                            pltpu.VMEM((N_MAPS, 1), F32),
                            pltpu.VMEM((N_MAPS, B_DV), F32)]),
        out_shape=jax.ShapeDtypeStruct((nb, B_HEADS, B_DV), F32),
        compiler_params=_params("parallel", "arbitrary"),
        name="attn_sample",
    )(page_table, lam.reshape(1), q.reshape(nb, 1, D_MODEL), k_new.reshape(nb, 1, D_MODEL),
      v_new.reshape(nb, B_HEADS, B_DV), slope_rows, subln.reshape(1, -1), *([ck] * g), *([cv] * g))
    return out.reshape(nb, D_MODEL)


def _diff_lambda(lam, layer_idx):
    lam_init = 0.8 - 0.6 * math.exp(-0.3 * layer_idx)
    lam = lam.astype(F32)
    val = jnp.exp(jnp.sum(lam[0] * lam[1])) - jnp.exp(jnp.sum(lam[2] * lam[3])) + lam_init
    return val, lam_init


def _norm_proj_kernel(x_ref, g_ref, *refs, layouts):
    n_w = len(layouts)
    w_refs, o_refs = refs[:n_w], list(refs[n_w:])
    xn = _rms(x_ref[...], g_ref[...]).astype(BF16)
    for w_ref, layout in zip(w_refs, layouts):
        y = _dot(xn, w_ref[...])
        if "n" in layout:
            o_refs.pop(0)[...] = y
        if "t" in layout:
            o_refs.pop(0)[0] = y.T


def _norm_proj(x, g, weights, layouts, nb, t):
    m = x.shape[0]
    tm = _row_tile(t)
    per_seq = t // tm
    nat = (pl.BlockSpec((tm, D_MODEL), lambda i: (i, 0)), jax.ShapeDtypeStruct((m, D_MODEL), F32))
    tra = (pl.BlockSpec((1, D_MODEL, tm), lambda i: (i // per_seq, 0, i % per_seq)),
           jax.ShapeDtypeStruct((nb, D_MODEL, t), F32))
    outs = [{"n": nat, "t": tra}[c] for layout in layouts for c in layout]
    return pl.pallas_call(
        functools.partial(_norm_proj_kernel, layouts=layouts),
        grid=(m // tm,),
        in_specs=[pl.BlockSpec((tm, D_MODEL), lambda i: (i, 0)), _resident((1, D_MODEL))]
                 + [_resident(w.shape) for w in weights],
        out_specs=[o[0] for o in outs],
        out_shape=[o[1] for o in outs],
        compiler_params=_params("parallel"),
        name="norm_proj",
    )(x, g.reshape(1, -1), *weights)


def _alibi_slopes():
    slopes = [2.0 ** (-8.0 * i / B_HEADS) for i in range(1, B_HEADS + 1)]
    assert all(float(np.asarray(s, dtype=BF16)) == s for s in slopes)
    return jnp.asarray(slopes, F32)


def _trunk(x, hgrn_state, project_kv, attend, w, *, chunk):
    nb, t, _ = x.shape
    m = nb * t
    x = x.reshape(m, D_MODEL)
    lbs = jnp.cumsum(jax.nn.softmax(w["a_lb"].astype(F32), axis=0), axis=0)
    slopes = _alibi_slopes()
    t_pad = -(-t // chunk) * chunk
    new_states = []
    kv = None
    for layer in range(DEPTH):
        g = w["norm_g"][layer]
        x = _swiglu_block(x, g[0], g[1], w["w_ffn_gu"][layer, 0], w["w_ffn_down"][layer, 0])
        if layer < N_A_LAYERS:
            proj = _norm_matmul(x, g[2], w["a_w_in"][layer], (A_FDIM, A_FDIM, D_MODEL, D_MODEL))
            proj = [a.reshape(nb, t, D_MODEL) for a in proj]
            if t_pad != t:
                proj = [jnp.pad(a, ((0, 0), (0, t_pad - t), (0, 0))) for a in proj]
            og, s_new = _hgrn_core(*proj, lbs[layer], w["a_gnorm"][layer], hgrn_state[layer],
                                   chunk=chunk, t_valid=t)
            new_states.append(s_new)
            x = _matmul_norm_res(og[:, :t].reshape(m, D_MODEL), w["a_w_out"][layer], g[3], x)
        else:
            j = layer - N_A_LAYERS
            lam, lam_init = _diff_lambda(w["b_lambda"][j], layer)
            oa = attend(x, g[2], w["b_w_q"][j], kv, slopes, lam, w["b_subln"][j], 1.0 - lam_init)
            x = _matmul_norm_res(oa.astype(BF16), w["b_w_o"][j], g[3], x)
        x = _swiglu_block(x, g[4], g[5], w["w_ffn_gu"][layer, 1], w["w_ffn_down"][layer, 1])
        if layer == N_A_LAYERS - 1:
            kv = project_kv(x)
    return x.reshape(nb, t, D_MODEL), jnp.stack(new_states), kv


def kernel(x_prompt, x_sample, cache_k, cache_v, state_hgrn, page_table, norm_g, w_ffn_gu, w_ffn_down,
           a_w_in, a_w_out, a_lb, a_gnorm, norm_kv, w_kv, b_w_q, b_w_o, b_lambda, b_subln):
    w = dict(norm_g=norm_g, a_lb=a_lb, a_gnorm=a_gnorm, norm_kv=norm_kv, b_lambda=b_lambda, b_subln=b_subln,
             w_ffn_gu=w_ffn_gu.astype(BF16), w_ffn_down=w_ffn_down.astype(BF16),
             a_w_in=a_w_in.astype(BF16), a_w_out=a_w_out.astype(BF16), w_kv=w_kv.astype(BF16),
             b_w_q=b_w_q.astype(BF16), b_w_o=b_w_o.astype(BF16))
    nb, t, _ = x_prompt.shape
    db, dt, _ = x_sample.shape

    w_k, w_v = w["w_kv"][:, :D_MODEL], w["w_kv"][:, D_MODEL:]

    def project_kv_prompt(x):
        return _norm_proj(x, norm_kv, [w_k, w_v], ["nt", "nt"], nb, t)

    def attend_prompt(x, g, w_q, kv, slopes, lam, subln, out_scale):
        (qt,) = _norm_proj(x, g, [w_q], ["t"], nb, t)
        k, _, _, vt = kv
        return _attn_prompt(qt, k.reshape(nb, t, D_MODEL), vt, slopes, lam, subln, out_scale).reshape(nb * t, D_MODEL)

    def project_kv_sample(x):
        return _norm_matmul(x, norm_kv, w["w_kv"], (D_MODEL, D_MODEL))

    def attend_sample(x, g, w_q, kv, slopes, lam, subln, out_scale):
        (q,) = _norm_matmul(x, g, w_q, (D_MODEL,))
        as_tok = lambda a: a.reshape(db, 1, D_MODEL)
        return _attn_sample(as_tok(q), as_tok(kv[0]), as_tok(kv[1]), cache_k, cache_v, page_table, slopes, lam,
                            subln, out_scale)

    s0_prompt = jnp.zeros((N_A_LAYERS, nb, A_HEADS, A_DK, A_DV), state_hgrn.dtype)
    y_p, st_p, (_, kt_p, v_p, _) = _trunk(x_prompt, s0_prompt, project_kv_prompt, attend_prompt, w, chunk=32)
    y_s, st_s, (k_s, v_s) = _trunk(x_sample, state_hgrn, project_kv_sample, attend_sample, w, chunk=BF16_SUBLANES)
    k_p = jnp.transpose(kt_p.reshape(nb, B_HEADS, 2, B_DH, t), (0, 4, 1, 2, 3))
    return (y_p, y_s, st_p,
            k_p, v_p.reshape(nb, t, B_HEADS, B_DV),
            st_s,
            k_s.reshape(db, dt, B_HEADS, 2, B_DH), v_s.reshape(db, dt, B_HEADS, B_DV))
```

```python
import functools
import math

import jax
import jax.numpy as jnp
import numpy as np
from jax import lax
from jax.experimental import pallas as pl
from jax.experimental.pallas import tpu as pltpu

D_MODEL = 1024
DEPTH = 2
N_A_LAYERS = DEPTH // 2
PAGE_SIZE = 128
A_HEADS = 8
A_DK = 128
A_DV = D_MODEL // A_HEADS
A_FDIM = A_HEADS * A_DK
B_HEADS = 8
B_DH = D_MODEL // (2 * B_HEADS)
B_DV = 2 * B_DH
FFN_DIM = 2816
NORM_EPS = 1e-6
MASK_VALUE = -1e30

LANES = 128
BF16_SUBLANES = 16
VMEM_LIMIT_BYTES = 48 * 1024 * 1024

F32 = jnp.float32
BF16 = jnp.bfloat16


def _rms(x, g):
    ms = jnp.mean(x * x, axis=-1, keepdims=True)
    return x * lax.rsqrt(ms + NORM_EPS) * g


def _silu(x):
    return x * jax.nn.sigmoid(x)


def _dot(a, b):
    return jnp.dot(a, b, preferred_element_type=F32)


def _dot_nt(a, b):
    return lax.dot_general(a, b, (((1,), (1,)), ((), ())), preferred_element_type=F32)


def _dot_tn(a, b):
    return lax.dot_general(a, b, (((0,), (0,)), ((), ())), preferred_element_type=F32)


def _resident(shape):
    nd = len(shape)
    return pl.BlockSpec(shape, lambda *_: (0,) * nd, pipeline_mode=pl.Buffered(1))


def _row_tile(m):
    return min(m, 512)


def _params(*sem):
    return pltpu.CompilerParams(dimension_semantics=sem, vmem_limit_bytes=VMEM_LIMIT_BYTES)


FFN_CHUNK = 256


def _swiglu_body(x_ref, gpre_ref, gpost_ref, wgu_ref, wd_ref, o_ref, acc_ref, after_chunk=None):
    x = x_ref[...]
    xn = _rms(x, gpre_ref[...]).astype(BF16)
    for c in range(FFN_DIM // FFN_CHUNK):
        lo = c * FFN_CHUNK
        gate = _dot(xn, wgu_ref[:, lo:lo + FFN_CHUNK])
        up = _dot(xn, wgu_ref[:, FFN_DIM + lo:FFN_DIM + lo + FFN_CHUNK])
        act = (_silu(gate) * up).astype(BF16)
        down = _dot(act, wd_ref[lo:lo + FFN_CHUNK, :])
        if c == 0:
            acc_ref[...] = down
        else:
            acc_ref[...] += down
        if after_chunk is not None:
            after_chunk(c)
    o_ref[...] = x + 0.5 * _rms(acc_ref[...], gpost_ref[...])


def _swiglu_kernel(x_ref, gpre_ref, gpost_ref, wgu_ref, wd_ref, o_ref, acc_ref):
    _swiglu_body(x_ref, gpre_ref, gpost_ref, wgu_ref, wd_ref, o_ref, acc_ref)


def _swiglu_block(x, g_pre, g_post, w_gu, w_down, which):
    m = x.shape[0]
    tm = _row_tile(m)
    row = pl.BlockSpec((tm, D_MODEL), lambda i: (i, 0))
    vec = _resident((1, D_MODEL))

    def picked(w):
        return pl.BlockSpec((None, None) + w.shape[2:], lambda i: which + (0, 0), pipeline_mode=pl.Buffered(1))

    return pl.pallas_call(
        _swiglu_kernel,
        grid=(m // tm,),
        in_specs=[row, vec, vec, picked(w_gu), picked(w_down)],
        out_specs=row,
        out_shape=jax.ShapeDtypeStruct((m, D_MODEL), F32),
        scratch_shapes=[pltpu.VMEM((tm, D_MODEL), F32)],
        compiler_params=_params("parallel"),
        name="swiglu_block",
    )(x, g_pre.reshape(1, -1), g_post.reshape(1, -1), w_gu, w_down)


def _norm_matmul_kernel(x_ref, g_ref, w_ref, *o_refs):
    xn = _rms(x_ref[...], g_ref[...]).astype(BF16)
    off = 0
    for o_ref in o_refs:
        n = o_ref.shape[-1]
        o_ref[...] = _dot(xn, w_ref[:, off:off + n])
        off += n


def _norm_matmul(x, g, w, splits):
    m = x.shape[0]
    tm = _row_tile(m)
    return pl.pallas_call(
        _norm_matmul_kernel,
        grid=(m // tm,),
        in_specs=[pl.BlockSpec((tm, D_MODEL), lambda i: (i, 0)), _resident((1, D_MODEL)), _resident(w.shape)],
        out_specs=[pl.BlockSpec((tm, n), lambda i: (i, 0)) for n in splits],
        out_shape=[jax.ShapeDtypeStruct((m, n), F32) for n in splits],
        compiler_params=_params("parallel"),
        name="norm_matmul",
    )(x, g.reshape(1, -1), w)


def _matmul_norm_res_kernel(a_ref, w_ref, g_ref, x_ref, o_ref):
    o_ref[...] = x_ref[...] + _rms(_dot(a_ref[...], w_ref[...]), g_ref[...])


def _matmul_norm_res(a, w, g, x):
    m = x.shape[0]
    tm = _row_tile(m)
    row = pl.BlockSpec((tm, D_MODEL), lambda i: (i, 0))
    return pl.pallas_call(
        _matmul_norm_res_kernel,
        grid=(m // tm,),
        in_specs=[row, _resident(w.shape), _resident((1, D_MODEL)), row],
        out_specs=row,
        out_shape=jax.ShapeDtypeStruct((m, D_MODEL), F32),
        compiler_params=_params("parallel"),
        name="matmul_norm_res",
    )(a, w, g.reshape(1, -1), x)


SUB = 16
LOG2_E = math.log2(math.e)


def _cumsum_rows(x, tri):
    hi = x.astype(BF16)
    r1 = x - hi.astype(F32)
    mid = r1.astype(BF16)
    lo = (r1 - mid.astype(F32)).astype(BF16)
    return _dot(tri, hi) + _dot(tri, mid) + _dot(tri, lo)


HGRN_TILE = 256


def _hgrn_kernel(q_ref, f_ref, i_ref, g_ref, lb_ref, gn_ref, s0_ref, o_ref, s_out_ref,
                 st_ref, b_ref, b2_ref, k_ref, *, chunk, t_valid):
    tile = q_ref.shape[1]
    n_sub = chunk // SUB
    ti = pl.program_id(1)
    heads = [slice(h * A_DK, (h + 1) * A_DK) for h in range(A_HEADS)]
    lb = lb_ref[...]
    gn = gn_ref[...]
    ri = lax.broadcasted_iota(jnp.int32, (chunk, chunk), 0)
    ci = lax.broadcasted_iota(jnp.int32, (chunk, chunk), 1)
    tri = jnp.where(ri >= ci, 1.0, 0.0).astype(BF16)
    half = SUB // 2
    sub_col = lax.broadcasted_iota(jnp.int32, (half, SUB), 1)
    causal_sub = (lax.broadcasted_iota(jnp.int32, (SUB, SUB), 0) >= lax.broadcasted_iota(jnp.int32, (SUB, SUB), 1))

    @pl.when(ti == 0)
    def _():
        for h in range(A_HEADS):
            st_ref[h] = s0_ref[0, h].T

    def body(c, carry):
        base = pl.multiple_of(c * chunk, chunk)
        rows = pl.ds(base, chunk)
        qq = _silu(q_ref[0, rows, :])
        forget = lb + (1.0 - lb) * jax.nn.sigmoid(f_ref[0, rows, :])
        logf = jnp.log(forget)
        kk = 1.0 - forget
        if t_valid % chunk:
            live = (ti * tile + base + lax.broadcasted_iota(jnp.int32, (chunk, 1), 0)) < t_valid
            qq = jnp.where(live, qq, 0.0)
            logf = jnp.where(live, logf, 0.0)
            kk = jnp.where(live, kk, 0.0)
        b = _cumsum_rows(logf, tri)
        b_ref[...] = b
        b2_ref[...] = b * LOG2_E
        k_ref[...] = kk
        vb = i_ref[0, rows, :].astype(BF16)
        b_last = b_ref[chunk - 1:chunk, :]
        qe = (qq * jnp.exp(b)).astype(BF16)
        kd = (kk * jnp.exp(b_last - b)).astype(BF16)
        decay = jnp.exp(b_last)
        gate = _silu(g_ref[0, rows, :])

        intra = []
        for i in range(n_sub):
            r0 = i * SUB
            qi = qq[r0:r0 + SUB]
            bi = b[r0:r0 + SUB]
            if i == 0:
                acc = [jnp.zeros((SUB, A_DV), F32) for _ in heads]
            else:
                ref_row = b_ref[r0 - 1:r0, :]
                qp = (qi * jnp.exp(bi - ref_row)).astype(BF16)
                kp = (kk[:r0] * jnp.exp(ref_row - b[:r0])).astype(BF16)
                acc = [_dot(_dot_nt(qp[:, hs], kp[:, hs]).astype(BF16), vb[:r0, hs]) for hs in heads]
            top = [jnp.zeros((half, SUB), F32) for _ in heads]
            bot = [jnp.zeros((half, SUB), F32) for _ in heads]
            b2i = bi * LOG2_E
            for s in range(SUB):
                first = 0 if s < half else half
                row_s = slice(r0 + s, r0 + s + 1)
                w = qi[first:] * jnp.exp2(jnp.minimum(b2i[first:] - b2_ref[row_s, :], 0.0)) * k_ref[row_s, :]
                for h, hs in enumerate(heads):
                    a = jnp.sum(w[:, hs], axis=-1, keepdims=True)
                    if s < half:
                        top[h] = jnp.where(sub_col == s, a[:half], top[h])
                    bot[h] = jnp.where(sub_col == s, a[-half:], bot[h])
            for h, hs in enumerate(heads):
                att = jnp.where(causal_sub, jnp.concatenate([top[h], bot[h]], axis=0), 0.0)
                acc[h] = acc[h] + _dot(att.astype(BF16), vb[r0:r0 + SUB, hs])
            intra.append(acc)

        for h, hs in enumerate(heads):
            st = st_ref[h]
            o = _dot_nt(qe[:, hs], st.astype(BF16)) + jnp.concatenate([acc[h] for acc in intra], axis=0)
            st_ref[h] = st * decay[:, hs] + _dot_tn(vb[:, hs], kd[:, hs])
            o_ref[0, rows, hs] = (_rms(o, gn) * gate[:, hs]).astype(BF16)
        return carry

    lax.fori_loop(0, tile // chunk, body, 0, unroll=2)

    @pl.when(ti == pl.num_programs(1) - 1)
    def _():
        for h in range(A_HEADS):
            s_out_ref[0, h] = st_ref[h].T


def _hgrn_core(q, f, i, g, lb, gnorm, s0, *, chunk, t_valid):
    nb, t_pad, _ = q.shape
    tile = min(HGRN_TILE, t_pad)
    seq = pl.BlockSpec((1, tile, D_MODEL), lambda b, t: (b, t, 0))
    state = pl.BlockSpec((1, A_HEADS, A_DK, A_DV), lambda b, t: (b, 0, 0, 0))
    kern = functools.partial(_hgrn_kernel, chunk=chunk, t_valid=t_valid)
    return pl.pallas_call(
        kern,
        grid=(nb, t_pad // tile),
        in_specs=[seq, seq, seq, seq,
                  pl.BlockSpec((1, A_FDIM), lambda b, t: (0, 0)),
                  pl.BlockSpec((1, A_DV), lambda b, t: (0, 0)),
                  state],
        out_specs=[seq, state],
        out_shape=[jax.ShapeDtypeStruct((nb, t_pad, D_MODEL), BF16),
                   jax.ShapeDtypeStruct((nb, A_HEADS, A_DK, A_DV), F32)],
        scratch_shapes=[pltpu.VMEM((A_HEADS, A_DV, A_DK), F32)] + [pltpu.VMEM((chunk, A_FDIM), F32)] * 3,
        compiler_params=_params("parallel", "arbitrary"),
        name="hgrn_core",
    )(q, f, i, g, lb.reshape(1, -1), gnorm.reshape(1, -1), s0)


ATTN_TILE = 512
POS_RADIX = 16


def _attn_prompt_kernel(slopes_ref, lam_ref, qt_ref, k_ref, vt_ref, sub_ref, o_ref, ka_ref, vtb_ref,
                        sa_ref, sb_ref, pa_ref, pb_ref, *, out_scale):
    h = pl.program_id(1)
    qi = pl.program_id(2)
    tq = qt_ref.shape[2]
    tk = tq
    n_kv = k_ref.shape[1] // tk

    @pl.when(qi == 0)
    def _():
        lane = lax.broadcasted_iota(jnp.int32, (tk, 2 * B_DH), 1)
        row = lax.broadcasted_iota(jnp.int32, (tk, 2 * B_DH), 0)
        for jt in range(n_kv):
            kpos = jt * tk + row
            digits = jnp.where(lane == B_DH, kpos // POS_RADIX, jnp.where(lane == B_DH + 1, kpos % POS_RADIX,
                               jnp.where(lane < B_DH + 4, 1, 0))).astype(F32)
            kblk = k_ref[0, jt * tk:(jt + 1) * tk, :]
            ka_ref[jt, 0] = jnp.where(lane < B_DH, kblk, digits).astype(BF16)
            ka_ref[jt, 1] = jnp.where(lane < B_DH, pltpu.roll(kblk, B_DH, axis=1), digits).astype(BF16)
            vtb_ref[jt] = vt_ref[0, :, jt * tk:(jt + 1) * tk].astype(BF16)

    slope = slopes_ref[h]
    lam = lam_ref[0]
    qt = qt_ref[0] * (B_DH ** -0.5)
    qpos = qi * tq + lax.broadcasted_iota(jnp.int32, (B_DH, tq), 1)
    r = lax.broadcasted_iota(jnp.int32, (B_DH, tq), 0)
    hi = (qpos // POS_RADIX).astype(F32)
    lo = (qpos % POS_RADIX).astype(F32)
    bias_rows = jnp.where(r == 0, POS_RADIX * slope, jnp.where(r == 1, slope,
                          jnp.where(r == 2, -POS_RADIX * slope * hi, jnp.where(r == 3, -slope * lo, 0.0))))
    qa = [jnp.concatenate([qt[j * B_DH:(j + 1) * B_DH], bias_rows], axis=0).astype(BF16) for j in range(2)]
    causal = (lax.broadcasted_iota(jnp.int32, (tk, tq), 0) <= lax.broadcasted_iota(jnp.int32, (tk, tq), 1))

    s_bufs, p_bufs = (sa_ref, sb_ref), (pa_ref, pb_ref)

    def put_scores(buf, tile, diagonal):
        for j in range(2):
            s = _dot(ka_ref[tile, j], qa[j])
            buf[j] = jnp.where(causal, s, MASK_VALUE) if diagonal else s

    def pipeline_step(slot, stats, accs, next_tile, prev_tile):
        if next_tile is not None:
            put_scores(s_bufs[1 - slot], next_tile, False)
        new_stats, new_accs = [], []
        for j in range(2):
            m, l = stats[j]
            pv = _dot(vtb_ref[prev_tile], p_bufs[1 - slot][j])
            s = s_bufs[slot][j]
            m_new = jnp.maximum(m, jnp.max(s, axis=0, keepdims=True))
            alpha = jnp.exp(m - m_new)
            p = jnp.exp(s - m_new)
            p_bufs[slot][j] = p.astype(BF16)
            new_stats.append((m_new, alpha * l + jnp.sum(p, axis=0, keepdims=True)))
            new_accs.append(alpha * (accs[j] + pv))
        return tuple(new_stats), tuple(new_accs)

    def final_step(slot, stats, accs, prev_tile):
        stats, accs = pipeline_step(slot, stats, accs, None, prev_tile)
        last_tile = jnp.maximum(qi - 1, 0)
        o1, o2 = [(accs[j] + _dot(vtb_ref[last_tile], p_bufs[slot][j])) / stats[j][1] for j in range(2)]
        ot = o1 - lam * o2
        ms = jnp.mean(ot * ot, axis=0, keepdims=True)
        ot = ot * lax.rsqrt(ms + NORM_EPS) * (sub_ref[...] * out_scale)
        o_ref[0] = ot.T.astype(BF16)

    put_scores(sa_ref, qi, True)
    pb_ref[...] = jnp.zeros(pb_ref.shape, BF16)

    def pair(i, carry):
        k = 2 * i
        stats, accs = pipeline_step(0, *carry, next_tile=k, prev_tile=jnp.maximum(k - 2, 0))
        return pipeline_step(1, stats, accs, next_tile=k + 1, prev_tile=jnp.where(k == 0, qi, k - 1))

    one = (jnp.full((1, tq), MASK_VALUE, F32), jnp.zeros((1, tq), F32))
    zero = jnp.zeros((B_DV, tq), F32)
    stats, accs = lax.fori_loop(0, qi // 2, pair, ((one, one), (zero, zero)))

    @pl.when(qi % 2 == 0)
    def _():
        final_step(0, stats, accs, jnp.maximum(qi - 2, 0))

    @pl.when(qi % 2 == 1)
    def _():
        k = qi - 1
        st, ac = pipeline_step(0, stats, accs, next_tile=k, prev_tile=jnp.maximum(k - 2, 0))
        final_step(1, st, ac, jnp.where(k == 0, qi, k - 1))


def _attn_prompt(qt, k, vt, slopes, lam, subln, out_scale):
    nb, t, _ = k.shape
    tq = min(ATTN_TILE, t)
    smem = pl.BlockSpec(memory_space=pltpu.SMEM)
    kern = functools.partial(_attn_prompt_kernel, out_scale=out_scale)
    return pl.pallas_call(
        kern,
        grid=(nb, B_HEADS, t // tq),
        in_specs=[smem, smem,
                  pl.BlockSpec((1, 2 * B_DH, tq), lambda b, h, i: (b, h, i)),
                  pl.BlockSpec((1, t, 2 * B_DH), lambda b, h, i: (b, 0, h)),
                  pl.BlockSpec((1, B_DV, t), lambda b, h, i: (b, h, 0)),
                  pl.BlockSpec((B_DV, 1), lambda b, h, i: (0, 0))],
        out_specs=pl.BlockSpec((1, tq, B_DV), lambda b, h, i: (b, i, h)),
        out_shape=jax.ShapeDtypeStruct((nb, t, D_MODEL), BF16),
        scratch_shapes=[pltpu.VMEM((t // tq, 2, tq, 2 * B_DH), BF16), pltpu.VMEM((t // tq, B_DV, tq), BF16),
                        pltpu.VMEM((2, tq, tq), F32), pltpu.VMEM((2, tq, tq), F32),
                        pltpu.VMEM((2, tq, tq), BF16), pltpu.VMEM((2, tq, tq), BF16)],
        compiler_params=_params("parallel", "parallel", "arbitrary"),
        name="attn_prompt",
    )(slopes, lam.reshape(1), qt, k, vt, subln.reshape(-1, 1))


N_MAPS = 2 * B_HEADS
PAGE_COLS = PAGE_SIZE * B_HEADS
PAGE_GROUP = 4
HOST_CALLS = 4
K_SLOTS = 2
V_SLOTS = 4


def _block_diag_q(q_row):
    col = lax.broadcasted_iota(jnp.int32, (N_MAPS, D_MODEL), 1)
    row = lax.broadcasted_iota(jnp.int32, (N_MAPS, D_MODEL), 0)
    mine = (col // B_DV == row % B_HEADS) & ((col // B_DH) % 2 == row // B_HEADS)
    return jnp.where(mine, q_row * (B_DH ** -0.5), 0.0)


def _swiglu_pages_kernel(pt_ref, x_ref, gpre_ref, gpost_ref, wgu_ref, wd_ref, q_ref, slope_ref, spread_ref,
                         m_in_ref, l_in_ref, acc_in_ref, ck_ref, cv_ref, o_ref, m_ref, l_ref, acc_ref,
                         ffn_acc_ref, kbuf_ref, vbuf_ref, ksem_ref, vsem_ref, s_ref, e_ref, pm_ref, mg_ref,
                         *, first_page, n_groups, n_pages):
    i = pl.program_id(0)

    def copies(g):
        out = []
        for j in range(PAGE_GROUP):
            page = pt_ref[i, first_page + g * PAGE_GROUP + j]
            ks, vs = g % K_SLOTS, g % V_SLOTS
            out.append(pltpu.make_async_copy(ck_ref.at[page], kbuf_ref.at[ks, j], ksem_ref.at[ks, j]))
            out.append(pltpu.make_async_copy(cv_ref.at[page], vbuf_ref.at[vs, j], vsem_ref.at[vs, j]))
        return out

    for cp in copies(0):
        cp.start()
    m_ref[...] = m_in_ref[...]
    l_ref[...] = l_in_ref[...]
    acc_ref[...] = acc_in_ref[...]
    qbd = _block_diag_q(q_ref[0]).astype(BF16)
    row_head = lax.broadcasted_iota(jnp.int32, (N_MAPS, 1), 0) % B_HEADS
    own_head = lax.broadcasted_iota(jnp.int32, (N_MAPS, PAGE_COLS), 1) % B_HEADS == row_head
    lane_pos = lax.broadcasted_iota(jnp.int32, (N_MAPS, PAGE_GROUP * PAGE_SIZE), 1)

    def scores(g):
        for cp in copies(g):
            cp.wait()
        if g + 1 < n_groups:
            for cp in copies(g + 1):
                cp.start()
        s = jnp.concatenate([_dot(qbd, kbuf_ref[g % K_SLOTS, j].astype(BF16)) for j in range(PAGE_GROUP)],
                            axis=1)
        kpos = (first_page + g * PAGE_GROUP) * PAGE_SIZE + lane_pos
        s_ref[g % 2] = s - slope_ref[...] * (n_pages * PAGE_SIZE - kpos).astype(F32)

    def softmax_step(g):
        s = s_ref[g % 2]
        m = m_ref[0]
        m_new = jnp.maximum(m, jnp.max(s, axis=-1, keepdims=True))
        alpha = jnp.exp(m - m_new)
        e = jnp.exp(s - m_new[:, :1])
        e_ref[g % 2] = e.astype(BF16)
        l_ref[0] = alpha * l_ref[0] + jnp.sum(e, axis=-1, keepdims=True)
        acc_ref[0] = alpha * acc_ref[0]
        m_ref[0] = m_new
        mg_ref[g % V_SLOTS] = m_new

    def spread_probs(g):
        e = e_ref[g % 2]
        for j in range(PAGE_GROUP):
            spread = _dot(e[:, j * PAGE_SIZE:(j + 1) * PAGE_SIZE], spread_ref[...])
            pm_ref[g % 2, j] = jnp.where(own_head, spread, 0.0).astype(BF16)

    def add_values(g):
        pv = jnp.zeros((N_MAPS, B_DV), F32)
        for j in range(PAGE_GROUP):
            pv = pv + _dot(pm_ref[g % 2, j], vbuf_ref[g % V_SLOTS, j].astype(BF16))
        acc_ref[0] = acc_ref[0] + jnp.exp(mg_ref[g % V_SLOTS] - m_ref[0]) * pv

    def after_chunk(c):
        for stage, g in ((add_values, c - 3), (spread_probs, c - 2), (softmax_step, c - 1), (scores, c)):
            if 0 <= g < n_groups:
                stage(g)

    _swiglu_body(x_ref, gpre_ref, gpost_ref, wgu_ref, wd_ref, o_ref, ffn_acc_ref, after_chunk)


def _swiglu_block_hosting_pages(x, g_pre, g_post, w_gu, w_down, which, q, slope_rows, state, ck, cv,
                                page_table, call_index):
    m_rows = x.shape[0]
    tm = _row_tile(m_rows)
    nb, n_pages = page_table.shape
    per_call = n_pages // HOST_CALLS
    n_groups = per_call // PAGE_GROUP
    assert m_rows // tm == nb and per_call * HOST_CALLS == n_pages and n_groups * PAGE_GROUP == per_call
    assert n_groups + 3 <= FFN_DIM // FFN_CHUNK
    tok = lax.broadcasted_iota(jnp.int32, (PAGE_SIZE, PAGE_COLS), 0)
    dst = lax.broadcasted_iota(jnp.int32, (PAGE_SIZE, PAGE_COLS), 1)
    spread = jnp.where(dst // B_HEADS == tok, 1.0, 0.0).astype(BF16)
    row = pl.BlockSpec((tm, D_MODEL), lambda i, pt: (i, 0))
    vec = _resident((1, D_MODEL))
    st = pl.BlockSpec((1, N_MAPS, B_DV), lambda i, pt: (i, 0, 0))
    any_space = pl.BlockSpec(memory_space=pl.ANY)

    def picked(w):
        return pl.BlockSpec((None, None) + w.shape[2:], lambda *_: which + (0, 0), pipeline_mode=pl.Buffered(1))

    kern = functools.partial(_swiglu_pages_kernel, first_page=call_index * per_call, n_groups=n_groups,
                             n_pages=n_pages)
    st_shape = jax.ShapeDtypeStruct((nb, N_MAPS, B_DV), F32)
    out = pl.pallas_call(
        kern,
        grid_spec=pltpu.PrefetchScalarGridSpec(
            num_scalar_prefetch=1,
            grid=(nb,),
            in_specs=[row, vec, vec, picked(w_gu), picked(w_down),
                      pl.BlockSpec((1, 1, D_MODEL), lambda i, pt: (i, 0, 0)),
                      _resident((N_MAPS, 1)), _resident(spread.shape), st, st, st, any_space, any_space],
            out_specs=[row, st, st, st],
            scratch_shapes=[pltpu.VMEM((tm, D_MODEL), F32),
                            pltpu.VMEM((K_SLOTS, PAGE_GROUP, D_MODEL, PAGE_SIZE), F32),
                            pltpu.VMEM((V_SLOTS, PAGE_GROUP, PAGE_COLS, B_DV), F32),
                            pltpu.SemaphoreType.DMA((K_SLOTS, PAGE_GROUP)),
                            pltpu.SemaphoreType.DMA((V_SLOTS, PAGE_GROUP)),
                            pltpu.VMEM((2, N_MAPS, PAGE_GROUP * PAGE_SIZE), F32),
                            pltpu.VMEM((2, N_MAPS, PAGE_GROUP * PAGE_SIZE), BF16),
                            pltpu.VMEM((2, PAGE_GROUP, N_MAPS, PAGE_COLS), BF16),
                            pltpu.VMEM((V_SLOTS, N_MAPS, B_DV), F32)]),
        out_shape=[jax.ShapeDtypeStruct((m_rows, D_MODEL), F32), st_shape, st_shape, st_shape],
        compiler_params=_params("arbitrary"),
        name="swiglu_block_hosting_pages",
    )(page_table, x, g_pre.reshape(1, -1), g_post.reshape(1, -1), w_gu, w_down, q, slope_rows, spread,
      *state, ck, cv)
    return out[0], tuple(out[1:])


def _attn_finish_kernel(lam_ref, q_ref, kn_ref, vn_ref, sub_ref, m_ref, l_ref, acc_ref, o_ref, *, out_scale):
    s_new = jnp.sum(_block_diag_q(q_ref[0]) * kn_ref[0], axis=-1, keepdims=True)
    m = m_ref[0][:, :1]
    m_new = jnp.maximum(m, s_new)
    alpha = jnp.exp(m - m_new)
    e = jnp.exp(s_new - m_new)
    vn = vn_ref[0]
    l = alpha * l_ref[0] + e
    acc = alpha * acc_ref[0] + e * jnp.concatenate([vn, vn], axis=0)
    o = acc / l
    o = o[:B_HEADS] - lam_ref[0] * o[B_HEADS:]
    o_ref[0] = _rms(o, sub_ref[...]) * out_scale


def _attn_sample_finish(q, k_new, v_new, state, lam, subln, out_scale):
    nb = q.shape[0]
    row = pl.BlockSpec((1, 1, D_MODEL), lambda b: (b, 0, 0))
    per_head = pl.BlockSpec((1, B_HEADS, B_DV), lambda b: (b, 0, 0))
    st = pl.BlockSpec((1, N_MAPS, B_DV), lambda b: (b, 0, 0))
    out = pl.pallas_call(
        functools.partial(_attn_finish_kernel, out_scale=out_scale),
        grid=(nb,),
        in_specs=[pl.BlockSpec(memory_space=pltpu.SMEM), row, row, per_head,
                  pl.BlockSpec((1, B_DV), lambda b: (0, 0)), st, st, st],
        out_specs=per_head,
        out_shape=jax.ShapeDtypeStruct((nb, B_HEADS, B_DV), F32),
        compiler_params=_params("parallel"),
        name="attn_sample_finish",
    )(lam.reshape(1), q, k_new, v_new.reshape(nb, B_HEADS, B_DV), subln.reshape(1, -1), *state)
    return out.reshape(nb, D_MODEL)


def _diff_lambda(lam, layer_idx):
    lam_init = 0.8 - 0.6 * math.exp(-0.3 * layer_idx)
    lam = lam.astype(F32)
    val = jnp.exp(jnp.sum(lam[0] * lam[1])) - jnp.exp(jnp.sum(lam[2] * lam[3])) + lam_init
    return val, lam_init


def _norm_proj_kernel(x_ref, g_ref, *refs, layouts):
    n_w = len(layouts)
    w_refs, o_refs = refs[:n_w], list(refs[n_w:])
    xn = _rms(x_ref[...], g_ref[...]).astype(BF16)
    for w_ref, layout in zip(w_refs, layouts):
        y = _dot(xn, w_ref[...])
        if "n" in layout:
            o_refs.pop(0)[...] = y
        if "t" in layout:
            o_refs.pop(0)[0] = y.T


def _norm_proj(x, g, weights, layouts, nb, t):
    m = x.shape[0]
    tm = _row_tile(t)
    per_seq = t // tm
    nat = (pl.BlockSpec((tm, D_MODEL), lambda i: (i, 0)), jax.ShapeDtypeStruct((m, D_MODEL), F32))
    tra = (pl.BlockSpec((1, D_MODEL, tm), lambda i: (i // per_seq, 0, i % per_seq)),
           jax.ShapeDtypeStruct((nb, D_MODEL, t), F32))
    outs = [{"n": nat, "t": tra}[c] for layout in layouts for c in layout]
    return pl.pallas_call(
        functools.partial(_norm_proj_kernel, layouts=layouts),
        grid=(m // tm,),
        in_specs=[pl.BlockSpec((tm, D_MODEL), lambda i: (i, 0)), _resident((1, D_MODEL))]
                 + [_resident(w.shape) for w in weights],
        out_specs=[o[0] for o in outs],
        out_shape=[o[1] for o in outs],
        compiler_params=_params("parallel"),
        name="norm_proj",
    )(x, g.reshape(1, -1), *weights)


def _alibi_slopes():
    slopes = [2.0 ** (-8.0 * i / B_HEADS) for i in range(1, B_HEADS + 1)]
    assert all(float(np.asarray(s, dtype=BF16)) == s for s in slopes)
    return jnp.asarray(slopes, F32)


ATTN_LAYER = N_A_LAYERS
assert DEPTH == 2


def _trunk_head(x, hgrn_state, project_kv, swiglu, w, *, chunk):
    nb, t, _ = x.shape
    m = nb * t
    x = x.reshape(m, D_MODEL)
    lbs = jnp.cumsum(jax.nn.softmax(w["a_lb"].astype(F32), axis=0), axis=0)
    t_pad = -(-t // chunk) * chunk
    g = w["norm_g"][0]
    x = swiglu(x, g[0], g[1], (0, 0))
    proj = _norm_matmul(x, g[2], w["a_w_in"][0], (A_FDIM, A_FDIM, D_MODEL, D_MODEL))
    proj = [a.reshape(nb, t, D_MODEL) for a in proj]
    if t_pad != t:
        proj = [jnp.pad(a, ((0, 0), (0, t_pad - t), (0, 0))) for a in proj]
    og, s_new = _hgrn_core(*proj, lbs[0], w["a_gnorm"][0], hgrn_state[0], chunk=chunk, t_valid=t)
    x = _matmul_norm_res(og[:, :t].reshape(m, D_MODEL), w["a_w_out"][0], g[3], x)
    x = swiglu(x, g[4], g[5], (0, 1))
    kv = project_kv(x)
    g = w["norm_g"][ATTN_LAYER]
    x = swiglu(x, g[0], g[1], (ATTN_LAYER, 0))
    return x, s_new[None], kv


def _trunk_tail(x, attn_out, swiglu, w):
    g = w["norm_g"][ATTN_LAYER]
    x = _matmul_norm_res(attn_out.astype(BF16), w["b_w_o"][0], g[3], x)
    return swiglu(x, g[4], g[5], (ATTN_LAYER, 1))


def kernel(x_prompt, x_sample, cache_k, cache_v, state_hgrn, page_table, norm_g, w_ffn_gu, w_ffn_down,
           a_w_in, a_w_out, a_lb, a_gnorm, norm_kv, w_kv, b_w_q, b_w_o, b_lambda, b_subln):
    w = dict(norm_g=norm_g, a_lb=a_lb, a_gnorm=a_gnorm,
             w_ffn_gu=w_ffn_gu.astype(BF16), w_ffn_down=w_ffn_down.astype(BF16),
             a_w_in=a_w_in.astype(BF16), a_w_out=a_w_out.astype(BF16), b_w_o=b_w_o.astype(BF16))
    w_kv, w_q = w_kv.astype(BF16), b_w_q.astype(BF16)[0]
    nb, t, _ = x_prompt.shape
    db, dt, _ = x_sample.shape
    n_phys = cache_k.shape[0]
    slopes = _alibi_slopes()
    lam, lam_init = _diff_lambda(b_lambda[0], ATTN_LAYER)
    out_scale = 1.0 - lam_init
    g_q = norm_g[ATTN_LAYER, 2]

    def plain_swiglu(x, g_pre, g_post, which):
        return _swiglu_block(x, g_pre, g_post, w["w_ffn_gu"], w["w_ffn_down"], which)

    x_s, st_s, (k_s, v_s) = _trunk_head(
        x_sample, state_hgrn, lambda x: _norm_matmul(x, norm_kv, w_kv, (D_MODEL, D_MODEL)), plain_swiglu, w,
        chunk=BF16_SUBLANES)
    (q_s,) = _norm_matmul(x_s, g_q, w_q, (D_MODEL,))
    as_tok = lambda a: a.reshape(db, 1, D_MODEL)

    ck = jnp.transpose(cache_k, (0, 2, 3, 4, 1)).reshape(n_phys, D_MODEL, PAGE_SIZE)
    cv = cache_v.reshape(n_phys, PAGE_COLS, B_DV)
    slope_rows = jnp.tile(slopes, 2).reshape(N_MAPS, 1)
    cache_state = [(jnp.full((db, N_MAPS, B_DV), MASK_VALUE, F32),) + (jnp.zeros((db, N_MAPS, B_DV), F32),) * 2]

    def hosting_swiglu(x, g_pre, g_post, which):
        call_index = len(cache_state) - 1
        y, state = _swiglu_block_hosting_pages(x, g_pre, g_post, w["w_ffn_gu"], w["w_ffn_down"], which,
                                               as_tok(q_s), slope_rows, cache_state[-1], ck, cv, page_table,
                                               call_index)
        cache_state.append(state)
        return y

    s0_prompt = jnp.zeros((N_A_LAYERS, nb, A_HEADS, A_DK, A_DV), state_hgrn.dtype)
    x_p, st_p, (k_nat, kt_p, v_p, vt_p) = _trunk_head(
        x_prompt, s0_prompt,
        lambda x: _norm_proj(x, norm_kv, [w_kv[:, :D_MODEL], w_kv[:, D_MODEL:]], ["nt", "nt"], nb, t),
        hosting_swiglu, w, chunk=32)
    (qt_p,) = _norm_proj(x_p, g_q, [w_q], ["t"], nb, t)
    o_p = _attn_prompt(qt_p, k_nat.reshape(nb, t, D_MODEL), vt_p, slopes, lam, b_subln[0], out_scale)
    y_p = _trunk_tail(x_p, o_p.reshape(nb * t, D_MODEL), hosting_swiglu, w).reshape(nb, t, D_MODEL)
    assert len(cache_state) == HOST_CALLS + 1

    o_s = _attn_sample_finish(as_tok(q_s), as_tok(k_s), as_tok(v_s), cache_state[-1], lam, b_subln[0], out_scale)
    y_s = _trunk_tail(x_s, o_s, plain_swiglu, w).reshape(db, dt, D_MODEL)

    k_p = jnp.transpose(kt_p.reshape(nb, B_HEADS, 2, B_DH, t), (0, 4, 1, 2, 3))
    return (y_p, y_s, st_p,
            k_p, v_p.reshape(nb, t, B_HEADS, B_DV),
            st_s,
            k_s.reshape(db, dt, B_HEADS, 2, B_DH), v_s.reshape(db, dt, B_HEADS, B_DV))
```

```python
import functools
import math

import jax
import jax.numpy as jnp
import numpy as np
from jax import lax
from jax.experimental import pallas as pl
from jax.experimental.pallas import tpu as pltpu

D_MODEL = 1024
DEPTH = 2
N_A_LAYERS = DEPTH // 2
PAGE_SIZE = 128
A_HEADS = 8
A_DK = 128
A_DV = D_MODEL // A_HEADS
A_FDIM = A_HEADS * A_DK
B_HEADS = 8
B_DH = D_MODEL // (2 * B_HEADS)
B_DV = 2 * B_DH
FFN_DIM = 2816
NORM_EPS = 1e-6
MASK_VALUE = -1e30

LANES = 128
BF16_SUBLANES = 16
VMEM_LIMIT_BYTES = 48 * 1024 * 1024

F32 = jnp.float32
BF16 = jnp.bfloat16


def _rms(x, g):
    ms = jnp.mean(x * x, axis=-1, keepdims=True)
    return x * lax.rsqrt(ms + NORM_EPS) * g


def _silu(x):
    return x * jax.nn.sigmoid(x)


def _dot(a, b):
    return jnp.dot(a, b, preferred_element_type=F32)


def _dot_nt(a, b):
    return lax.dot_general(a, b, (((1,), (1,)), ((), ())), preferred_element_type=F32)


def _dot_tn(a, b):
    return lax.dot_general(a, b, (((0,), (0,)), ((), ())), preferred_element_type=F32)


def _resident(shape):
    nd = len(shape)
    return pl.BlockSpec(shape, lambda *_: (0,) * nd, pipeline_mode=pl.Buffered(1))


def _row_tile(m):
    return min(m, 512)


def _params(*sem):
    return pltpu.CompilerParams(dimension_semantics=sem, vmem_limit_bytes=VMEM_LIMIT_BYTES)


FFN_CHUNK = 256


def _swiglu_body(x_ref, gpre_ref, gpost_ref, wgu_ref, wd_ref, o_ref, acc_ref, after_chunk=None):
    x = x_ref[...]
    xn = _rms(x, gpre_ref[...]).astype(BF16)
    for c in range(FFN_DIM // FFN_CHUNK):
        lo = c * FFN_CHUNK
        gate = _dot(xn, wgu_ref[:, lo:lo + FFN_CHUNK])
        up = _dot(xn, wgu_ref[:, FFN_DIM + lo:FFN_DIM + lo + FFN_CHUNK])
        act = (_silu(gate) * up).astype(BF16)
        down = _dot(act, wd_ref[lo:lo + FFN_CHUNK, :])
        if c == 0:
            acc_ref[...] = down
        else:
            acc_ref[...] += down
        if after_chunk is not None:
            after_chunk(c)
    o_ref[...] = x + 0.5 * _rms(acc_ref[...], gpost_ref[...])


def _swiglu_kernel(x_ref, gpre_ref, gpost_ref, wgu_ref, wd_ref, o_ref, acc_ref):
    _swiglu_body(x_ref, gpre_ref, gpost_ref, wgu_ref, wd_ref, o_ref, acc_ref)


def _swiglu_block(x, g_pre, g_post, w_gu, w_down, which):
    m = x.shape[0]
    tm = _row_tile(m)
    row = pl.BlockSpec((tm, D_MODEL), lambda i: (i, 0))
    vec = _resident((1, D_MODEL))

    def picked(w):
        return pl.BlockSpec((None, None) + w.shape[2:], lambda i: which + (0, 0), pipeline_mode=pl.Buffered(1))

    return pl.pallas_call(
        _swiglu_kernel,
        grid=(m // tm,),
        in_specs=[row, vec, vec, picked(w_gu), picked(w_down)],
        out_specs=row,
        out_shape=jax.ShapeDtypeStruct((m, D_MODEL), F32),
        scratch_shapes=[pltpu.VMEM((tm, D_MODEL), F32)],
        compiler_params=_params("parallel"),
        name="swiglu_block",
    )(x, g_pre.reshape(1, -1), g_post.reshape(1, -1), w_gu, w_down)


def _norm_matmul_kernel(x_ref, g_ref, w_ref, *o_refs):
    xn = _rms(x_ref[...], g_ref[...]).astype(BF16)
    off = 0
    for o_ref in o_refs:
        n = o_ref.shape[-1]
        o_ref[...] = _dot(xn, w_ref[:, off:off + n])
        off += n


def _norm_matmul(x, g, w, splits):
    m = x.shape[0]
    tm = _row_tile(m)
    return pl.pallas_call(
        _norm_matmul_kernel,
        grid=(m // tm,),
        in_specs=[pl.BlockSpec((tm, D_MODEL), lambda i: (i, 0)), _resident((1, D_MODEL)), _resident(w.shape)],
        out_specs=[pl.BlockSpec((tm, n), lambda i: (i, 0)) for n in splits],
        out_shape=[jax.ShapeDtypeStruct((m, n), F32) for n in splits],
        compiler_params=_params("parallel"),
        name="norm_matmul",
    )(x, g.reshape(1, -1), w)


def _matmul_norm_res_kernel(a_ref, w_ref, g_ref, x_ref, o_ref):
    o_ref[...] = x_ref[...] + _rms(_dot(a_ref[...], w_ref[...]), g_ref[...])


def _matmul_norm_res(a, w, g, x):
    m = x.shape[0]
    tm = _row_tile(m)
    row = pl.BlockSpec((tm, D_MODEL), lambda i: (i, 0))
    return pl.pallas_call(
        _matmul_norm_res_kernel,
        grid=(m // tm,),
        in_specs=[row, _resident(w.shape), _resident((1, D_MODEL)), row],
        out_specs=row,
        out_shape=jax.ShapeDtypeStruct((m, D_MODEL), F32),
        compiler_params=_params("parallel"),
        name="matmul_norm_res",
    )(a, w, g.reshape(1, -1), x)


SUB = 16
LOG2_E = math.log2(math.e)


def _cumsum_rows(x, tri):
    hi = x.astype(BF16)
    r1 = x - hi.astype(F32)
    mid = r1.astype(BF16)
    lo = (r1 - mid.astype(F32)).astype(BF16)
    return _dot(tri, hi) + _dot(tri, mid) + _dot(tri, lo)


HGRN_TILE = 256


def _hgrn_kernel(q_ref, f_ref, i_ref, g_ref, lb_ref, gn_ref, s0_ref, o_ref, s_out_ref,
                 st_ref, b_ref, b2_ref, k_ref, *, chunk, t_valid):
    tile = q_ref.shape[1]
    n_sub = chunk // SUB
    ti = pl.program_id(1)
    heads = [slice(h * A_DK, (h + 1) * A_DK) for h in range(A_HEADS)]
    lb = lb_ref[...]
    gn = gn_ref[...]
    ri = lax.broadcasted_iota(jnp.int32, (chunk, chunk), 0)
    ci = lax.broadcasted_iota(jnp.int32, (chunk, chunk), 1)
    tri = jnp.where(ri >= ci, 1.0, 0.0).astype(BF16)
    half = SUB // 2
    sub_col = lax.broadcasted_iota(jnp.int32, (half, SUB), 1)
    causal_sub = (lax.broadcasted_iota(jnp.int32, (SUB, SUB), 0) >= lax.broadcasted_iota(jnp.int32, (SUB, SUB), 1))

    @pl.when(ti == 0)
    def _():
        for h in range(A_HEADS):
            st_ref[h] = s0_ref[0, h].T

    def body(c, carry):
        base = pl.multiple_of(c * chunk, chunk)
        rows = pl.ds(base, chunk)
        qq = _silu(q_ref[0, rows, :])
        forget = lb + (1.0 - lb) * jax.nn.sigmoid(f_ref[0, rows, :])
        logf = jnp.log(forget)
        kk = 1.0 - forget
        if t_valid % chunk:
            live = (ti * tile + base + lax.broadcasted_iota(jnp.int32, (chunk, 1), 0)) < t_valid
            qq = jnp.where(live, qq, 0.0)
            logf = jnp.where(live, logf, 0.0)
            kk = jnp.where(live, kk, 0.0)
        b = _cumsum_rows(logf, tri)
        b_ref[...] = b
        b2_ref[...] = b * LOG2_E
        k_ref[...] = kk
        vb = i_ref[0, rows, :].astype(BF16)
        b_last = b_ref[chunk - 1:chunk, :]
        qe = (qq * jnp.exp(b)).astype(BF16)
        kd = (kk * jnp.exp(b_last - b)).astype(BF16)
        decay = jnp.exp(b_last)
        gate = _silu(g_ref[0, rows, :])

        intra = []
        for i in range(n_sub):
            r0 = i * SUB
            qi = qq[r0:r0 + SUB]
            bi = b[r0:r0 + SUB]
            if i == 0:
                acc = [jnp.zeros((SUB, A_DV), F32) for _ in heads]
            else:
                ref_row = b_ref[r0 - 1:r0, :]
                qp = (qi * jnp.exp(bi - ref_row)).astype(BF16)
                kp = (kk[:r0] * jnp.exp(ref_row - b[:r0])).astype(BF16)
                acc = [_dot(_dot_nt(qp[:, hs], kp[:, hs]).astype(BF16), vb[:r0, hs]) for hs in heads]
            top = [jnp.zeros((half, SUB), F32) for _ in heads]
            bot = [jnp.zeros((half, SUB), F32) for _ in heads]
            b2i = bi * LOG2_E
            for s in range(SUB):
                first = 0 if s < half else half
                row_s = slice(r0 + s, r0 + s + 1)
                w = qi[first:] * jnp.exp2(jnp.minimum(b2i[first:] - b2_ref[row_s, :], 0.0)) * k_ref[row_s, :]
                for h, hs in enumerate(heads):
                    a = jnp.sum(w[:, hs], axis=-1, keepdims=True)
                    if s < half:
                        top[h] = jnp.where(sub_col == s, a[:half], top[h])
                    bot[h] = jnp.where(sub_col == s, a[-half:], bot[h])
            for h, hs in enumerate(heads):
                att = jnp.where(causal_sub, jnp.concatenate([top[h], bot[h]], axis=0), 0.0)
                acc[h] = acc[h] + _dot(att.astype(BF16), vb[r0:r0 + SUB, hs])
            intra.append(acc)

        for h, hs in enumerate(heads):
            st = st_ref[h]
            o = _dot_nt(qe[:, hs], st.astype(BF16)) + jnp.concatenate([acc[h] for acc in intra], axis=0)
            st_ref[h] = st * decay[:, hs] + _dot_tn(vb[:, hs], kd[:, hs])
            o_ref[0, rows, hs] = (_rms(o, gn) * gate[:, hs]).astype(BF16)
        return carry

    lax.fori_loop(0, tile // chunk, body, 0, unroll=2)

    @pl.when(ti == pl.num_programs(1) - 1)
    def _():
        for h in range(A_HEADS):
            s_out_ref[0, h] = st_ref[h].T


def _hgrn_core(q, f, i, g, lb, gnorm, s0, *, chunk, t_valid):
    nb, t_pad, _ = q.shape
    tile = min(HGRN_TILE, t_pad)
    seq = pl.BlockSpec((1, tile, D_MODEL), lambda b, t: (b, t, 0))
    state = pl.BlockSpec((1, A_HEADS, A_DK, A_DV), lambda b, t: (b, 0, 0, 0))
    kern = functools.partial(_hgrn_kernel, chunk=chunk, t_valid=t_valid)
    return pl.pallas_call(
        kern,
        grid=(nb, t_pad // tile),
        in_specs=[seq, seq, seq, seq,
                  pl.BlockSpec((1, A_FDIM), lambda b, t: (0, 0)),
                  pl.BlockSpec((1, A_DV), lambda b, t: (0, 0)),
                  state],
        out_specs=[seq, state],
        out_shape=[jax.ShapeDtypeStruct((nb, t_pad, D_MODEL), BF16),
                   jax.ShapeDtypeStruct((nb, A_HEADS, A_DK, A_DV), F32)],
        scratch_shapes=[pltpu.VMEM((A_HEADS, A_DV, A_DK), F32)] + [pltpu.VMEM((chunk, A_FDIM), F32)] * 3,
        compiler_params=_params("parallel", "arbitrary"),
        name="hgrn_core",
    )(q, f, i, g, lb.reshape(1, -1), gnorm.reshape(1, -1), s0)


ATTN_TILE = 512
POS_RADIX = 16


def _attn_prompt_kernel(slopes_ref, lam_ref, qt_ref, k_ref, vt_ref, sub_ref, o_ref, ka_ref, vtb_ref,
                        sa_ref, sb_ref, pa_ref, pb_ref, *, out_scale):
    h = pl.program_id(1)
    qi = pl.program_id(2)
    tq = qt_ref.shape[2]
    tk = tq
    n_kv = k_ref.shape[1] // tk

    @pl.when(qi == 0)
    def _():
        lane = lax.broadcasted_iota(jnp.int32, (tk, 2 * B_DH), 1)
        row = lax.broadcasted_iota(jnp.int32, (tk, 2 * B_DH), 0)
        for jt in range(n_kv):
            kpos = jt * tk + row
            digits = jnp.where(lane == B_DH, kpos // POS_RADIX, jnp.where(lane == B_DH + 1, kpos % POS_RADIX,
                               jnp.where(lane < B_DH + 4, 1, 0))).astype(F32)
            kblk = k_ref[0, jt * tk:(jt + 1) * tk, :]
            ka_ref[jt, 0] = jnp.where(lane < B_DH, kblk, digits).astype(BF16)
            ka_ref[jt, 1] = jnp.where(lane < B_DH, pltpu.roll(kblk, B_DH, axis=1), digits).astype(BF16)
            vtb_ref[jt] = vt_ref[0, :, jt * tk:(jt + 1) * tk].astype(BF16)

    slope = slopes_ref[h]
    lam = lam_ref[0]
    qt = qt_ref[0] * (B_DH ** -0.5)
    qpos = qi * tq + lax.broadcasted_iota(jnp.int32, (B_DH, tq), 1)
    r = lax.broadcasted_iota(jnp.int32, (B_DH, tq), 0)
    hi = (qpos // POS_RADIX).astype(F32)
    lo = (qpos % POS_RADIX).astype(F32)
    bias_rows = jnp.where(r == 0, POS_RADIX * slope, jnp.where(r == 1, slope,
                          jnp.where(r == 2, -POS_RADIX * slope * hi, jnp.where(r == 3, -slope * lo, 0.0))))
    qa = [jnp.concatenate([qt[j * B_DH:(j + 1) * B_DH], bias_rows], axis=0).astype(BF16) for j in range(2)]
    causal = (lax.broadcasted_iota(jnp.int32, (tk, tq), 0) <= lax.broadcasted_iota(jnp.int32, (tk, tq), 1))

    s_bufs, p_bufs = (sa_ref, sb_ref), (pa_ref, pb_ref)

    def put_scores(buf, tile, diagonal):
        for j in range(2):
            s = _dot(ka_ref[tile, j], qa[j])
            buf[j] = jnp.where(causal, s, MASK_VALUE) if diagonal else s

    def pipeline_step(slot, stats, accs, next_tile, prev_tile):
        if next_tile is not None:
            put_scores(s_bufs[1 - slot], next_tile, False)
        new_stats, new_accs = [], []
        for j in range(2):
            m, l = stats[j]
            pv = _dot(vtb_ref[prev_tile], p_bufs[1 - slot][j])
            s = s_bufs[slot][j]
            m_new = jnp.maximum(m, jnp.max(s, axis=0, keepdims=True))
            alpha = jnp.exp(m - m_new)
            p = jnp.exp(s - m_new)
            p_bufs[slot][j] = p.astype(BF16)
            new_stats.append((m_new, alpha * l + jnp.sum(p, axis=0, keepdims=True)))
            new_accs.append(alpha * (accs[j] + pv))
        return tuple(new_stats), tuple(new_accs)

    def final_step(slot, stats, accs, prev_tile):
        stats, accs = pipeline_step(slot, stats, accs, None, prev_tile)
        last_tile = jnp.maximum(qi - 1, 0)
        o1, o2 = [(accs[j] + _dot(vtb_ref[last_tile], p_bufs[slot][j])) / stats[j][1] for j in range(2)]
        ot = o1 - lam * o2
        ms = jnp.mean(ot * ot, axis=0, keepdims=True)
        ot = ot * lax.rsqrt(ms + NORM_EPS) * (sub_ref[...] * out_scale)
        o_ref[0] = ot.T.astype(BF16)

    put_scores(sa_ref, qi, True)
    pb_ref[...] = jnp.zeros(pb_ref.shape, BF16)

    def pair(i, carry):
        k = 2 * i
        stats, accs = pipeline_step(0, *carry, next_tile=k, prev_tile=jnp.maximum(k - 2, 0))
        return pipeline_step(1, stats, accs, next_tile=k + 1, prev_tile=jnp.where(k == 0, qi, k - 1))

    one = (jnp.full((1, tq), MASK_VALUE, F32), jnp.zeros((1, tq), F32))
    zero = jnp.zeros((B_DV, tq), F32)
    stats, accs = lax.fori_loop(0, qi // 2, pair, ((one, one), (zero, zero)))

    @pl.when(qi % 2 == 0)
    def _():
        final_step(0, stats, accs, jnp.maximum(qi - 2, 0))

    @pl.when(qi % 2 == 1)
    def _():
        k = qi - 1
        st, ac = pipeline_step(0, stats, accs, next_tile=k, prev_tile=jnp.maximum(k - 2, 0))
        final_step(1, st, ac, jnp.where(k == 0, qi, k - 1))


def _attn_prompt(qt, k, vt, slopes, lam, subln, out_scale):
    nb, t, _ = k.shape
    tq = min(ATTN_TILE, t)
    smem = pl.BlockSpec(memory_space=pltpu.SMEM)
    kern = functools.partial(_attn_prompt_kernel, out_scale=out_scale)
    return pl.pallas_call(
        kern,
        grid=(nb, B_HEADS, t // tq),
        in_specs=[smem, smem,
                  pl.BlockSpec((1, 2 * B_DH, tq), lambda b, h, i: (b, h, i)),
                  pl.BlockSpec((1, t, 2 * B_DH), lambda b, h, i: (b, 0, h)),
                  pl.BlockSpec((1, B_DV, t), lambda b, h, i: (b, h, 0)),
                  pl.BlockSpec((B_DV, 1), lambda b, h, i: (0, 0))],
        out_specs=pl.BlockSpec((1, tq, B_DV), lambda b, h, i: (b, i, h)),
        out_shape=jax.ShapeDtypeStruct((nb, t, D_MODEL), BF16),
        scratch_shapes=[pltpu.VMEM((t // tq, 2, tq, 2 * B_DH), BF16), pltpu.VMEM((t // tq, B_DV, tq), BF16),
                        pltpu.VMEM((2, tq, tq), F32), pltpu.VMEM((2, tq, tq), F32),
                        pltpu.VMEM((2, tq, tq), BF16), pltpu.VMEM((2, tq, tq), BF16)],
        compiler_params=_params("parallel", "parallel", "arbitrary"),
        name="attn_prompt",
    )(slopes, lam.reshape(1), qt, k, vt, subln.reshape(-1, 1))


N_MAPS = 2 * B_HEADS
PAGE_COLS = PAGE_SIZE * B_HEADS
PAGE_GROUP = 4
HOST_CALLS = 4
PAGE_LOOKAHEAD = 2
K_SLOTS = PAGE_LOOKAHEAD + 1
V_SLOTS = PAGE_LOOKAHEAD + 3
HOSTING_VMEM_LIMIT_BYTES = 56 * 1024 * 1024


def _block_diag_q(q_row):
    col = lax.broadcasted_iota(jnp.int32, (N_MAPS, D_MODEL), 1)
    row = lax.broadcasted_iota(jnp.int32, (N_MAPS, D_MODEL), 0)
    mine = (col // B_DV == row % B_HEADS) & ((col // B_DH) % 2 == row // B_HEADS)
    return jnp.where(mine, q_row * (B_DH ** -0.5), 0.0)


def _swiglu_pages_kernel(pt_ref, x_ref, gpre_ref, gpost_ref, wgu_ref, wd_ref, q_ref, slope_ref, spread_ref,
                         m_in_ref, l_in_ref, acc_in_ref, ck_ref, cv_ref, o_ref, m_ref, l_ref, acc_ref,
                         ffn_acc_ref, kbuf_ref, vbuf_ref, ksem_ref, vsem_ref, s_ref, e_ref, pm_ref, mg_ref,
                         *, first_page, n_groups, n_pages):
    i = pl.program_id(0)
    n_ahead = min(PAGE_LOOKAHEAD, n_groups)

    def copies(g, step=None):
        out = []
        for j in range(PAGE_GROUP):
            page = pt_ref[i if step is None else step, first_page + g * PAGE_GROUP + j]
            ks, vs = g % K_SLOTS, g % V_SLOTS
            out.append(pltpu.make_async_copy(ck_ref.at[page], kbuf_ref.at[ks, j], ksem_ref.at[ks, j]))
            out.append(pltpu.make_async_copy(cv_ref.at[page], vbuf_ref.at[vs, j], vsem_ref.at[vs, j]))
        return out

    def slots_free_after_chunk(g):
        last_v = max([u for u in range(n_groups) if u % V_SLOTS == g % V_SLOTS], default=-3)
        last_k = max([u for u in range(n_groups) if u % K_SLOTS == g % K_SLOTS], default=0)
        return max(last_v + 3, last_k, 0)

    @pl.when(i == 0)
    def _():
        for g in range(n_ahead):
            for cp in copies(g):
                cp.start()

    m_ref[...] = m_in_ref[...]
    l_ref[...] = l_in_ref[...]
    acc_ref[...] = acc_in_ref[...]
    qbd = _block_diag_q(q_ref[0]).astype(BF16)
    row_head = lax.broadcasted_iota(jnp.int32, (N_MAPS, 1), 0) % B_HEADS
    own_head = lax.broadcasted_iota(jnp.int32, (N_MAPS, PAGE_COLS), 1) % B_HEADS == row_head
    lane_pos = lax.broadcasted_iota(jnp.int32, (N_MAPS, PAGE_GROUP * PAGE_SIZE), 1)

    def scores(g):
        for cp in copies(g):
            cp.wait()
        if g + PAGE_LOOKAHEAD < n_groups:
            for cp in copies(g + PAGE_LOOKAHEAD):
                cp.start()
        s = jnp.concatenate([_dot(qbd, kbuf_ref[g % K_SLOTS, j].astype(BF16)) for j in range(PAGE_GROUP)],
                            axis=1)
        kpos = (first_page + g * PAGE_GROUP) * PAGE_SIZE + lane_pos
        s_ref[g % 2] = s - slope_ref[...] * (n_pages * PAGE_SIZE - kpos).astype(F32)

    def softmax_step(g):
        s = s_ref[g % 2]
        m = m_ref[0]
        m_new = jnp.maximum(m, jnp.max(s, axis=-1, keepdims=True))
        alpha = jnp.exp(m - m_new)
        e = jnp.exp(s - m_new[:, :1])
        e_ref[g % 2] = e.astype(BF16)
        l_ref[0] = alpha * l_ref[0] + jnp.sum(e, axis=-1, keepdims=True)
        acc_ref[0] = alpha * acc_ref[0]
        m_ref[0] = m_new
        mg_ref[g % V_SLOTS] = m_new

    def spread_probs(g):
        e = e_ref[g % 2]
        for j in range(PAGE_GROUP):
            spread = _dot(e[:, j * PAGE_SIZE:(j + 1) * PAGE_SIZE], spread_ref[...])
            pm_ref[g % 2, j] = jnp.where(own_head, spread, 0.0).astype(BF16)

    def add_values(g):
        pv = jnp.zeros((N_MAPS, B_DV), F32)
        for j in range(PAGE_GROUP):
            pv = pv + _dot(pm_ref[g % 2, j], vbuf_ref[g % V_SLOTS, j].astype(BF16))
        acc_ref[0] = acc_ref[0] + jnp.exp(mg_ref[g % V_SLOTS] - m_ref[0]) * pv

    def after_chunk(c):
        for stage, g in ((add_values, c - 3), (spread_probs, c - 2), (softmax_step, c - 1), (scores, c)):
            if 0 <= g < n_groups:
                stage(g)
        for g in range(n_ahead):
            if slots_free_after_chunk(g) == c:
                @pl.when(i + 1 < pl.num_programs(0))
                def _():
                    for cp in copies(g, i + 1):
                        cp.start()

    assert all(slots_free_after_chunk(g) < FFN_DIM // FFN_CHUNK for g in range(n_ahead))
    _swiglu_body(x_ref, gpre_ref, gpost_ref, wgu_ref, wd_ref, o_ref, ffn_acc_ref, after_chunk)


def _swiglu_block_hosting_pages(x, g_pre, g_post, w_gu, w_down, which, q, slope_rows, state, ck, cv,
                                page_table, call_index):
    m_rows = x.shape[0]
    tm = _row_tile(m_rows)
    nb, n_pages = page_table.shape
    per_call = n_pages // HOST_CALLS
    n_groups = per_call // PAGE_GROUP
    assert m_rows // tm == nb and per_call * HOST_CALLS == n_pages and n_groups * PAGE_GROUP == per_call
    assert n_groups + 3 <= FFN_DIM // FFN_CHUNK
    tok = lax.broadcasted_iota(jnp.int32, (PAGE_SIZE, PAGE_COLS), 0)
    dst = lax.broadcasted_iota(jnp.int32, (PAGE_SIZE, PAGE_COLS), 1)
    spread = jnp.where(dst // B_HEADS == tok, 1.0, 0.0).astype(BF16)
    row = pl.BlockSpec((tm, D_MODEL), lambda i, pt: (i, 0))
    vec = _resident((1, D_MODEL))
    st = pl.BlockSpec((1, N_MAPS, B_DV), lambda i, pt: (i, 0, 0))
    any_space = pl.BlockSpec(memory_space=pl.ANY)

    def picked(w):
        return pl.BlockSpec((None, None) + w.shape[2:], lambda *_: which + (0, 0), pipeline_mode=pl.Buffered(1))

    kern = functools.partial(_swiglu_pages_kernel, first_page=call_index * per_call, n_groups=n_groups,
                             n_pages=n_pages)
    st_shape = jax.ShapeDtypeStruct((nb, N_MAPS, B_DV), F32)
    out = pl.pallas_call(
        kern,
        grid_spec=pltpu.PrefetchScalarGridSpec(
            num_scalar_prefetch=1,
            grid=(nb,),
            in_specs=[row, vec, vec, picked(w_gu), picked(w_down),
                      pl.BlockSpec((1, 1, D_MODEL), lambda i, pt: (i, 0, 0)),
                      _resident((N_MAPS, 1)), _resident(spread.shape), st, st, st, any_space, any_space],
            out_specs=[row, st, st, st],
            scratch_shapes=[pltpu.VMEM((tm, D_MODEL), F32),
                            pltpu.VMEM((K_SLOTS, PAGE_GROUP, D_MODEL, PAGE_SIZE), F32),
                            pltpu.VMEM((V_SLOTS, PAGE_GROUP, PAGE_COLS, B_DV), F32),
                            pltpu.SemaphoreType.DMA((K_SLOTS, PAGE_GROUP)),
                            pltpu.SemaphoreType.DMA((V_SLOTS, PAGE_GROUP)),
                            pltpu.VMEM((2, N_MAPS, PAGE_GROUP * PAGE_SIZE), F32),
                            pltpu.VMEM((2, N_MAPS, PAGE_GROUP * PAGE_SIZE), BF16),
                            pltpu.VMEM((2, PAGE_GROUP, N_MAPS, PAGE_COLS), BF16),
                            pltpu.VMEM((V_SLOTS, N_MAPS, B_DV), F32)]),
        out_shape=[jax.ShapeDtypeStruct((m_rows, D_MODEL), F32), st_shape, st_shape, st_shape],
        compiler_params=pltpu.CompilerParams(dimension_semantics=("arbitrary",),
                                             vmem_limit_bytes=HOSTING_VMEM_LIMIT_BYTES),
        name="swiglu_block_hosting_pages",
    )(page_table, x, g_pre.reshape(1, -1), g_post.reshape(1, -1), w_gu, w_down, q, slope_rows, spread,
      *state, ck, cv)
    return out[0], tuple(out[1:])


def _attn_finish_kernel(lam_ref, q_ref, kn_ref, vn_ref, sub_ref, m_ref, l_ref, acc_ref, o_ref, *, out_scale):
    s_new = jnp.sum(_block_diag_q(q_ref[0]) * kn_ref[0], axis=-1, keepdims=True)
    m = m_ref[0][:, :1]
    m_new = jnp.maximum(m, s_new)
    alpha = jnp.exp(m - m_new)
    e = jnp.exp(s_new - m_new)
    vn = vn_ref[0]
    l = alpha * l_ref[0] + e
    acc = alpha * acc_ref[0] + e * jnp.concatenate([vn, vn], axis=0)
    o = acc / l
    o = o[:B_HEADS] - lam_ref[0] * o[B_HEADS:]
    o_ref[0] = _rms(o, sub_ref[...]) * out_scale


def _attn_sample_finish(q, k_new, v_new, state, lam, subln, out_scale):
    nb = q.shape[0]
    row = pl.BlockSpec((1, 1, D_MODEL), lambda b: (b, 0, 0))
    per_head = pl.BlockSpec((1, B_HEADS, B_DV), lambda b: (b, 0, 0))
    st = pl.BlockSpec((1, N_MAPS, B_DV), lambda b: (b, 0, 0))
    out = pl.pallas_call(
        functools.partial(_attn_finish_kernel, out_scale=out_scale),
        grid=(nb,),
        in_specs=[pl.BlockSpec(memory_space=pltpu.SMEM), row, row, per_head,
                  pl.BlockSpec((1, B_DV), lambda b: (0, 0)), st, st, st],
        out_specs=per_head,
        out_shape=jax.ShapeDtypeStruct((nb, B_HEADS, B_DV), F32),
        compiler_params=_params("parallel"),
        name="attn_sample_finish",
    )(lam.reshape(1), q, k_new, v_new.reshape(nb, B_HEADS, B_DV), subln.reshape(1, -1), *state)
    return out.reshape(nb, D_MODEL)


def _diff_lambda(lam, layer_idx):
    lam_init = 0.8 - 0.6 * math.exp(-0.3 * layer_idx)
    lam = lam.astype(F32)
    val = jnp.exp(jnp.sum(lam[0] * lam[1])) - jnp.exp(jnp.sum(lam[2] * lam[3])) + lam_init
    return val, lam_init


def _norm_proj_kernel(x_ref, g_ref, *refs, layouts):
    n_w = len(layouts)
    w_refs, o_refs = refs[:n_w], list(refs[n_w:])
    xn = _rms(x_ref[...], g_ref[...]).astype(BF16)
    for w_ref, layout in zip(w_refs, layouts):
        y = _dot(xn, w_ref[...])
        if "n" in layout:
            o_refs.pop(0)[...] = y
        if "t" in layout:
            o_refs.pop(0)[0] = y.T


def _norm_proj(x, g, weights, layouts, nb, t):
    m = x.shape[0]
    tm = _row_tile(t)
    per_seq = t // tm
    nat = (pl.BlockSpec((tm, D_MODEL), lambda i: (i, 0)), jax.ShapeDtypeStruct((m, D_MODEL), F32))
    tra = (pl.BlockSpec((1, D_MODEL, tm), lambda i: (i // per_seq, 0, i % per_seq)),
           jax.ShapeDtypeStruct((nb, D_MODEL, t), F32))
    outs = [{"n": nat, "t": tra}[c] for layout in layouts for c in layout]
    return pl.pallas_call(
        functools.partial(_norm_proj_kernel, layouts=layouts),
        grid=(m // tm,),
        in_specs=[pl.BlockSpec((tm, D_MODEL), lambda i: (i, 0)), _resident((1, D_MODEL))]
                 + [_resident(w.shape) for w in weights],
        out_specs=[o[0] for o in outs],
        out_shape=[o[1] for o in outs],
        compiler_params=_params("parallel"),
        name="norm_proj",
    )(x, g.reshape(1, -1), *weights)


def _alibi_slopes():
    slopes = [2.0 ** (-8.0 * i / B_HEADS) for i in range(1, B_HEADS + 1)]
    assert all(float(np.asarray(s, dtype=BF16)) == s for s in slopes)
    return jnp.asarray(slopes, F32)


ATTN_LAYER = N_A_LAYERS
assert DEPTH == 2


def _trunk_head(x, hgrn_state, project_kv, swiglu, w, *, chunk):
    nb, t, _ = x.shape
    m = nb * t
    x = x.reshape(m, D_MODEL)
    lbs = jnp.cumsum(jax.nn.softmax(w["a_lb"].astype(F32), axis=0), axis=0)
    t_pad = -(-t // chunk) * chunk
    g = w["norm_g"][0]
    x = swiglu(x, g[0], g[1], (0, 0))
    proj = _norm_matmul(x, g[2], w["a_w_in"][0], (A_FDIM, A_FDIM, D_MODEL, D_MODEL))
    proj = [a.reshape(nb, t, D_MODEL) for a in proj]
    if t_pad != t:
        proj = [jnp.pad(a, ((0, 0), (0, t_pad - t), (0, 0))) for a in proj]
    og, s_new = _hgrn_core(*proj, lbs[0], w["a_gnorm"][0], hgrn_state[0], chunk=chunk, t_valid=t)
    x = _matmul_norm_res(og[:, :t].reshape(m, D_MODEL), w["a_w_out"][0], g[3], x)
    x = swiglu(x, g[4], g[5], (0, 1))
    kv = project_kv(x)
    g = w["norm_g"][ATTN_LAYER]
    x = swiglu(x, g[0], g[1], (ATTN_LAYER, 0))
    return x, s_new[None], kv


def _trunk_tail(x, attn_out, swiglu, w):
    g = w["norm_g"][ATTN_LAYER]
    x = _matmul_norm_res(attn_out.astype(BF16), w["b_w_o"][0], g[3], x)
    return swiglu(x, g[4], g[5], (ATTN_LAYER, 1))


def kernel(x_prompt, x_sample, cache_k, cache_v, state_hgrn, page_table, norm_g, w_ffn_gu, w_ffn_down,
           a_w_in, a_w_out, a_lb, a_gnorm, norm_kv, w_kv, b_w_q, b_w_o, b_lambda, b_subln):
    w = dict(norm_g=norm_g, a_lb=a_lb, a_gnorm=a_gnorm,
             w_ffn_gu=w_ffn_gu.astype(BF16), w_ffn_down=w_ffn_down.astype(BF16),
             a_w_in=a_w_in.astype(BF16), a_w_out=a_w_out.astype(BF16), b_w_o=b_w_o.astype(BF16))
    w_kv, w_q = w_kv.astype(BF16), b_w_q.astype(BF16)[0]
    nb, t, _ = x_prompt.shape
    db, dt, _ = x_sample.shape
    n_phys = cache_k.shape[0]
    slopes = _alibi_slopes()
    lam, lam_init = _diff_lambda(b_lambda[0], ATTN_LAYER)
    out_scale = 1.0 - lam_init
    g_q = norm_g[ATTN_LAYER, 2]

    def plain_swiglu(x, g_pre, g_post, which):
        return _swiglu_block(x, g_pre, g_post, w["w_ffn_gu"], w["w_ffn_down"], which)

    x_s, st_s, (k_s, v_s) = _trunk_head(
        x_sample, state_hgrn, lambda x: _norm_matmul(x, norm_kv, w_kv, (D_MODEL, D_MODEL)), plain_swiglu, w,
        chunk=BF16_SUBLANES)
    (q_s,) = _norm_matmul(x_s, g_q, w_q, (D_MODEL,))
    as_tok = lambda a: a.reshape(db, 1, D_MODEL)

    ck = jnp.transpose(cache_k, (0, 2, 3, 4, 1)).reshape(n_phys, D_MODEL, PAGE_SIZE)
    cv = cache_v.reshape(n_phys, PAGE_COLS, B_DV)
    slope_rows = jnp.tile(slopes, 2).reshape(N_MAPS, 1)
    cache_state = [(jnp.full((db, N_MAPS, B_DV), MASK_VALUE, F32),) + (jnp.zeros((db, N_MAPS, B_DV), F32),) * 2]

    def hosting_swiglu(x, g_pre, g_post, which):
        call_index = len(cache_state) - 1
        y, state = _swiglu_block_hosting_pages(x, g_pre, g_post, w["w_ffn_gu"], w["w_ffn_down"], which,
                                               as_tok(q_s), slope_rows, cache_state[-1], ck, cv, page_table,
                                               call_index)
        cache_state.append(state)
        return y

    s0_prompt = jnp.zeros((N_A_LAYERS, nb, A_HEADS, A_DK, A_DV), state_hgrn.dtype)
    x_p, st_p, (k_nat, kt_p, v_p, vt_p) = _trunk_head(
        x_prompt, s0_prompt,
        lambda x: _norm_proj(x, norm_kv, [w_kv[:, :D_MODEL], w_kv[:, D_MODEL:]], ["nt", "nt"], nb, t),
        hosting_swiglu, w, chunk=32)
    (qt_p,) = _norm_proj(x_p, g_q, [w_q], ["t"], nb, t)
    o_p = _attn_prompt(qt_p, k_nat.reshape(nb, t, D_MODEL), vt_p, slopes, lam, b_subln[0], out_scale)
    y_p = _trunk_tail(x_p, o_p.reshape(nb * t, D_MODEL), hosting_swiglu, w).reshape(nb, t, D_MODEL)
    assert len(cache_state) == HOST_CALLS + 1

    o_s = _attn_sample_finish(as_tok(q_s), as_tok(k_s), as_tok(v_s), cache_state[-1], lam, b_subln[0], out_scale)
    y_s = _trunk_tail(x_s, o_s, plain_swiglu, w).reshape(db, dt, D_MODEL)

    k_p = jnp.transpose(kt_p.reshape(nb, B_HEADS, 2, B_DH, t), (0, 4, 1, 2, 3))
    return (y_p, y_s, st_p,
            k_p, v_p.reshape(nb, t, B_HEADS, B_DV),
            st_s,
            k_s.reshape(db, dt, B_HEADS, 2, B_DH), v_s.reshape(db, dt, B_HEADS, B_DV))
```

```python
import functools
import math

import jax
import jax.numpy as jnp
import numpy as np
from jax import lax
from jax.experimental import pallas as pl
from jax.experimental.pallas import tpu as pltpu

D_MODEL = 1024
DEPTH = 2
N_A_LAYERS = DEPTH // 2
PAGE_SIZE = 128
A_HEADS = 8
A_DK = 128
A_DV = D_MODEL // A_HEADS
A_FDIM = A_HEADS * A_DK
B_HEADS = 8
B_DH = D_MODEL // (2 * B_HEADS)
B_DV = 2 * B_DH
FFN_DIM = 2816
NORM_EPS = 1e-6
MASK_VALUE = -1e30

LANES = 128
BF16_SUBLANES = 16
VMEM_LIMIT_BYTES = 48 * 1024 * 1024

F32 = jnp.float32
BF16 = jnp.bfloat16


def _rms(x, g):
    ms = jnp.mean(x * x, axis=-1, keepdims=True)
    return x * lax.rsqrt(ms + NORM_EPS) * g


def _silu(x):
    return x * jax.nn.sigmoid(x)


def _dot(a, b):
    return jnp.dot(a, b, preferred_element_type=F32)


def _dot_nt(a, b):
    return lax.dot_general(a, b, (((1,), (1,)), ((), ())), preferred_element_type=F32)


def _dot_tn(a, b):
    return lax.dot_general(a, b, (((0,), (0,)), ((), ())), preferred_element_type=F32)


def _resident(shape):
    nd = len(shape)
    return pl.BlockSpec(shape, lambda *_: (0,) * nd, pipeline_mode=pl.Buffered(1))


def _row_tile(m):
    return min(m, 512)


def _params(*sem):
    return pltpu.CompilerParams(dimension_semantics=sem, vmem_limit_bytes=VMEM_LIMIT_BYTES)


FFN_CHUNK = 256


def _swiglu_body(x_ref, gpre_ref, gpost_ref, wgu_ref, wd_ref, o_ref, acc_ref, after_chunk=None):
    x = x_ref[...]
    xn = _rms(x, gpre_ref[...]).astype(BF16)
    for c in range(FFN_DIM // FFN_CHUNK):
        lo = c * FFN_CHUNK
        gate = _dot(xn, wgu_ref[:, lo:lo + FFN_CHUNK])
        up = _dot(xn, wgu_ref[:, FFN_DIM + lo:FFN_DIM + lo + FFN_CHUNK])
        act = (_silu(gate) * up).astype(BF16)
        down = _dot(act, wd_ref[lo:lo + FFN_CHUNK, :])
        if c == 0:
            acc_ref[...] = down
        else:
            acc_ref[...] += down
        if after_chunk is not None:
            after_chunk(c)
    o_ref[...] = x + 0.5 * _rms(acc_ref[...], gpost_ref[...])


def _swiglu_kernel(x_ref, gpre_ref, gpost_ref, wgu_ref, wd_ref, o_ref, acc_ref):
    _swiglu_body(x_ref, gpre_ref, gpost_ref, wgu_ref, wd_ref, o_ref, acc_ref)


def _swiglu_block(x, g_pre, g_post, w_gu, w_down, which):
    m = x.shape[0]
    tm = _row_tile(m)
    row = pl.BlockSpec((tm, D_MODEL), lambda i: (i, 0))
    vec = _resident((1, D_MODEL))

    def picked(w):
        return pl.BlockSpec((None, None) + w.shape[2:], lambda i: which + (0, 0), pipeline_mode=pl.Buffered(1))

    return pl.pallas_call(
        _swiglu_kernel,
        grid=(m // tm,),
        in_specs=[row, vec, vec, picked(w_gu), picked(w_down)],
        out_specs=row,
        out_shape=jax.ShapeDtypeStruct((m, D_MODEL), F32),
        scratch_shapes=[pltpu.VMEM((tm, D_MODEL), F32)],
        compiler_params=_params("parallel"),
        name="swiglu_block",
    )(x, g_pre.reshape(1, -1), g_post.reshape(1, -1), w_gu, w_down)


def _norm_matmul_kernel(x_ref, g_ref, w_ref, *o_refs):
    xn = _rms(x_ref[...], g_ref[...]).astype(BF16)
    off = 0
    for o_ref in o_refs:
        n = o_ref.shape[-1]
        o_ref[...] = _dot(xn, w_ref[:, off:off + n])
        off += n


def _norm_matmul(x, g, w, splits):
    m = x.shape[0]
    tm = _row_tile(m)
    return pl.pallas_call(
        _norm_matmul_kernel,
        grid=(m // tm,),
        in_specs=[pl.BlockSpec((tm, D_MODEL), lambda i: (i, 0)), _resident((1, D_MODEL)), _resident(w.shape)],
        out_specs=[pl.BlockSpec((tm, n), lambda i: (i, 0)) for n in splits],
        out_shape=[jax.ShapeDtypeStruct((m, n), F32) for n in splits],
        compiler_params=_params("parallel"),
        name="norm_matmul",
    )(x, g.reshape(1, -1), w)


def _matmul_norm_res_kernel(a_ref, w_ref, g_ref, x_ref, o_ref):
    o_ref[...] = x_ref[...] + _rms(_dot(a_ref[...], w_ref[...]), g_ref[...])


def _matmul_norm_res(a, w, g, x):
    m = x.shape[0]
    tm = _row_tile(m)
    row = pl.BlockSpec((tm, D_MODEL), lambda i: (i, 0))
    return pl.pallas_call(
        _matmul_norm_res_kernel,
        grid=(m // tm,),
        in_specs=[row, _resident(w.shape), _resident((1, D_MODEL)), row],
        out_specs=row,
        out_shape=jax.ShapeDtypeStruct((m, D_MODEL), F32),
        compiler_params=_params("parallel"),
        name="matmul_norm_res",
    )(a, w, g.reshape(1, -1), x)


SUB = 16
LOG2_E = math.log2(math.e)


def _cumsum_rows(x, tri):
    hi = x.astype(BF16)
    r1 = x - hi.astype(F32)
    mid = r1.astype(BF16)
    lo = (r1 - mid.astype(F32)).astype(BF16)
    return _dot(tri, hi) + _dot(tri, mid) + _dot(tri, lo)


HGRN_TILE = 256


def _hgrn_kernel(q_ref, f_ref, i_ref, g_ref, lb_ref, gn_ref, s0_ref, x_ref, wout_ref, gpost_ref, o_ref, s_out_ref,
                 st_ref, b_ref, b2_ref, k_ref, mix_ref, *, chunk, t_valid):
    tile = q_ref.shape[1]
    n_sub = chunk // SUB
    ti = pl.program_id(1)
    heads = [slice(h * A_DK, (h + 1) * A_DK) for h in range(A_HEADS)]
    lb = lb_ref[...]
    gn = gn_ref[...]
    ri = lax.broadcasted_iota(jnp.int32, (chunk, chunk), 0)
    ci = lax.broadcasted_iota(jnp.int32, (chunk, chunk), 1)
    tri = jnp.where(ri >= ci, 1.0, 0.0).astype(BF16)
    half = SUB // 2
    sub_col = lax.broadcasted_iota(jnp.int32, (half, SUB), 1)
    causal_sub = (lax.broadcasted_iota(jnp.int32, (SUB, SUB), 0) >= lax.broadcasted_iota(jnp.int32, (SUB, SUB), 1))

    @pl.when(ti == 0)
    def _():
        for h in range(A_HEADS):
            st_ref[h] = s0_ref[0, h].T

    def body(c, carry):
        base = pl.multiple_of(c * chunk, chunk)
        rows = pl.ds(base, chunk)
        qq = _silu(q_ref[0, rows, :])
        forget = lb + (1.0 - lb) * jax.nn.sigmoid(f_ref[0, rows, :])
        logf = jnp.log(forget)
        kk = 1.0 - forget
        if t_valid % chunk:
            live = (ti * tile + base + lax.broadcasted_iota(jnp.int32, (chunk, 1), 0)) < t_valid
            qq = jnp.where(live, qq, 0.0)
            logf = jnp.where(live, logf, 0.0)
            kk = jnp.where(live, kk, 0.0)
        b = _cumsum_rows(logf, tri)
        b_ref[...] = b
        b2_ref[...] = b * LOG2_E
        k_ref[...] = kk
        vb = i_ref[0, rows, :].astype(BF16)
        b_last = b_ref[chunk - 1:chunk, :]
        qe = (qq * jnp.exp(b)).astype(BF16)
        kd = (kk * jnp.exp(b_last - b)).astype(BF16)
        decay = jnp.exp(b_last)
        gate = _silu(g_ref[0, rows, :])

        intra = []
        for i in range(n_sub):
            r0 = i * SUB
            qi = qq[r0:r0 + SUB]
            bi = b[r0:r0 + SUB]
            if i == 0:
                acc = [jnp.zeros((SUB, A_DV), F32) for _ in heads]
            else:
                ref_row = b_ref[r0 - 1:r0, :]
                qp = (qi * jnp.exp(bi - ref_row)).astype(BF16)
                kp = (kk[:r0] * jnp.exp(ref_row - b[:r0])).astype(BF16)
                acc = [_dot(_dot_nt(qp[:, hs], kp[:, hs]).astype(BF16), vb[:r0, hs]) for hs in heads]
            top = [jnp.zeros((half, SUB), F32) for _ in heads]
            bot = [jnp.zeros((half, SUB), F32) for _ in heads]
            b2i = bi * LOG2_E
            for s in range(SUB):
                first = 0 if s < half else half
                row_s = slice(r0 + s, r0 + s + 1)
                w = qi[first:] * jnp.exp2(jnp.minimum(b2i[first:] - b2_ref[row_s, :], 0.0)) * k_ref[row_s, :]
                for h, hs in enumerate(heads):
                    a = jnp.sum(w[:, hs], axis=-1, keepdims=True)
                    if s < half:
                        top[h] = jnp.where(sub_col == s, a[:half], top[h])
                    bot[h] = jnp.where(sub_col == s, a[-half:], bot[h])
            for h, hs in enumerate(heads):
                att = jnp.where(causal_sub, jnp.concatenate([top[h], bot[h]], axis=0), 0.0)
                acc[h] = acc[h] + _dot(att.astype(BF16), vb[r0:r0 + SUB, hs])
            intra.append(acc)

        for h, hs in enumerate(heads):
            st = st_ref[h]
            o = _dot_nt(qe[:, hs], st.astype(BF16)) + jnp.concatenate([acc[h] for acc in intra], axis=0)
            st_ref[h] = st * decay[:, hs] + _dot_tn(vb[:, hs], kd[:, hs])
            mix_ref[rows, hs] = (_rms(o, gn) * gate[:, hs]).astype(BF16)
        return carry

    lax.fori_loop(0, tile // chunk, body, 0, unroll=2)
    o_ref[0] = x_ref[0] + _rms(_dot(mix_ref[...], wout_ref[...]), gpost_ref[...])

    @pl.when(ti == pl.num_programs(1) - 1)
    def _():
        for h in range(A_HEADS):
            s_out_ref[0, h] = st_ref[h].T


def _hgrn_mixer(q, f, i, g, lb, gnorm, s0, x, w_out, g_post, *, chunk, t_valid):
    nb, t_pad, _ = q.shape
    tile = min(HGRN_TILE, t_pad)
    seq = pl.BlockSpec((1, tile, D_MODEL), lambda b, t: (b, t, 0))
    state = pl.BlockSpec((1, A_HEADS, A_DK, A_DV), lambda b, t: (b, 0, 0, 0))
    kern = functools.partial(_hgrn_kernel, chunk=chunk, t_valid=t_valid)
    return pl.pallas_call(
        kern,
        grid=(nb, t_pad // tile),
        in_specs=[seq, seq, seq, seq,
                  pl.BlockSpec((1, A_FDIM), lambda b, t: (0, 0)),
                  pl.BlockSpec((1, A_DV), lambda b, t: (0, 0)),
                  state, seq, _resident(w_out.shape), _resident((1, D_MODEL))],
        out_specs=[seq, state],
        out_shape=[jax.ShapeDtypeStruct((nb, t_pad, D_MODEL), F32),
                   jax.ShapeDtypeStruct((nb, A_HEADS, A_DK, A_DV), F32)],
        scratch_shapes=[pltpu.VMEM((A_HEADS, A_DV, A_DK), F32)] + [pltpu.VMEM((chunk, A_FDIM), F32)] * 3
                       + [pltpu.VMEM((tile, D_MODEL), BF16)],
        compiler_params=_params("parallel", "arbitrary"),
        name="hgrn_mixer",
    )(q, f, i, g, lb.reshape(1, -1), gnorm.reshape(1, -1), s0, x, w_out, g_post.reshape(1, -1))


ATTN_TILE = 512
POS_RADIX = 16


def _attn_prompt_kernel(slopes_ref, lam_ref, qt_ref, k_ref, vt_ref, sub_ref, o_ref, ka_ref, vtb_ref,
                        sa_ref, sb_ref, pa_ref, pb_ref, *, out_scale):
    h = pl.program_id(1)
    qi = pl.program_id(2)
    tq = qt_ref.shape[2]
    tk = tq
    n_kv = k_ref.shape[1] // tk

    @pl.when(qi == 0)
    def _():
        lane = lax.broadcasted_iota(jnp.int32, (tk, 2 * B_DH), 1)
        row = lax.broadcasted_iota(jnp.int32, (tk, 2 * B_DH), 0)
        for jt in range(n_kv):
            kpos = jt * tk + row
            digits = jnp.where(lane == B_DH, kpos // POS_RADIX, jnp.where(lane == B_DH + 1, kpos % POS_RADIX,
                               jnp.where(lane < B_DH + 4, 1, 0))).astype(F32)
            kblk = k_ref[0, jt * tk:(jt + 1) * tk, :]
            ka_ref[jt, 0] = jnp.where(lane < B_DH, kblk, digits).astype(BF16)
            ka_ref[jt, 1] = jnp.where(lane < B_DH, pltpu.roll(kblk, B_DH, axis=1), digits).astype(BF16)
            vtb_ref[jt] = vt_ref[0, :, jt * tk:(jt + 1) * tk].astype(BF16)

    slope = slopes_ref[h]
    lam = lam_ref[0]
    qt = qt_ref[0] * (B_DH ** -0.5)
    qpos = qi * tq + lax.broadcasted_iota(jnp.int32, (B_DH, tq), 1)
    r = lax.broadcasted_iota(jnp.int32, (B_DH, tq), 0)
    hi = (qpos // POS_RADIX).astype(F32)
    lo = (qpos % POS_RADIX).astype(F32)
    bias_rows = jnp.where(r == 0, POS_RADIX * slope, jnp.where(r == 1, slope,
                          jnp.where(r == 2, -POS_RADIX * slope * hi, jnp.where(r == 3, -slope * lo, 0.0))))
    qa = [jnp.concatenate([qt[j * B_DH:(j + 1) * B_DH], bias_rows], axis=0).astype(BF16) for j in range(2)]
    causal = (lax.broadcasted_iota(jnp.int32, (tk, tq), 0) <= lax.broadcasted_iota(jnp.int32, (tk, tq), 1))

    s_bufs, p_bufs = (sa_ref, sb_ref), (pa_ref, pb_ref)

    def put_scores(buf, tile, diagonal):
        for j in range(2):
            s = _dot(ka_ref[tile, j], qa[j])
            buf[j] = jnp.where(causal, s, MASK_VALUE) if diagonal else s

    def pipeline_step(slot, stats, accs, next_tile, prev_tile):
        if next_tile is not None:
            put_scores(s_bufs[1 - slot], next_tile, False)
        new_stats, new_accs = [], []
        for j in range(2):
            m, l = stats[j]
            pv = _dot(vtb_ref[prev_tile], p_bufs[1 - slot][j])
            s = s_bufs[slot][j]
            m_new = jnp.maximum(m, jnp.max(s, axis=0, keepdims=True))
            alpha = jnp.exp(m - m_new)
            p = jnp.exp(s - m_new)
            p_bufs[slot][j] = p.astype(BF16)
            new_stats.append((m_new, alpha * l + jnp.sum(p, axis=0, keepdims=True)))
            new_accs.append(alpha * (accs[j] + pv))
        return tuple(new_stats), tuple(new_accs)

    def final_step(slot, stats, accs, prev_tile):
        stats, accs = pipeline_step(slot, stats, accs, None, prev_tile)
        last_tile = jnp.maximum(qi - 1, 0)
        o1, o2 = [(accs[j] + _dot(vtb_ref[last_tile], p_bufs[slot][j])) / stats[j][1] for j in range(2)]
        ot = o1 - lam * o2
        ms = jnp.mean(ot * ot, axis=0, keepdims=True)
        ot = ot * lax.rsqrt(ms + NORM_EPS) * (sub_ref[...] * out_scale)
        o_ref[0] = ot.T.astype(BF16)

    put_scores(sa_ref, qi, True)
    pb_ref[...] = jnp.zeros(pb_ref.shape, BF16)

    def pair(i, carry):
        k = 2 * i
        stats, accs = pipeline_step(0, *carry, next_tile=k, prev_tile=jnp.maximum(k - 2, 0))
        return pipeline_step(1, stats, accs, next_tile=k + 1, prev_tile=jnp.where(k == 0, qi, k - 1))

    one = (jnp.full((1, tq), MASK_VALUE, F32), jnp.zeros((1, tq), F32))
    zero = jnp.zeros((B_DV, tq), F32)
    stats, accs = lax.fori_loop(0, qi // 2, pair, ((one, one), (zero, zero)))

    @pl.when(qi % 2 == 0)
    def _():
        final_step(0, stats, accs, jnp.maximum(qi - 2, 0))

    @pl.when(qi % 2 == 1)
    def _():
        k = qi - 1
        st, ac = pipeline_step(0, stats, accs, next_tile=k, prev_tile=jnp.maximum(k - 2, 0))
        final_step(1, st, ac, jnp.where(k == 0, qi, k - 1))


def _attn_prompt(qt, k, vt, slopes, lam, subln, out_scale):
    nb, t, _ = k.shape
    tq = min(ATTN_TILE, t)
    smem = pl.BlockSpec(memory_space=pltpu.SMEM)
    kern = functools.partial(_attn_prompt_kernel, out_scale=out_scale)
    return pl.pallas_call(
        kern,
        grid=(nb, B_HEADS, t // tq),
        in_specs=[smem, smem,
                  pl.BlockSpec((1, 2 * B_DH, tq), lambda b, h, i: (b, h, i)),
                  pl.BlockSpec((1, t, 2 * B_DH), lambda b, h, i: (b, 0, h)),
                  pl.BlockSpec((1, B_DV, t), lambda b, h, i: (b, h, 0)),
                  pl.BlockSpec((B_DV, 1), lambda b, h, i: (0, 0))],
        out_specs=pl.BlockSpec((1, tq, B_DV), lambda b, h, i: (b, i, h)),
        out_shape=jax.ShapeDtypeStruct((nb, t, D_MODEL), BF16),
        scratch_shapes=[pltpu.VMEM((t // tq, 2, tq, 2 * B_DH), BF16), pltpu.VMEM((t // tq, B_DV, tq), BF16),
                        pltpu.VMEM((2, tq, tq), F32), pltpu.VMEM((2, tq, tq), F32),
                        pltpu.VMEM((2, tq, tq), BF16), pltpu.VMEM((2, tq, tq), BF16)],
        compiler_params=_params("parallel", "parallel", "arbitrary"),
        name="attn_prompt",
    )(slopes, lam.reshape(1), qt, k, vt, subln.reshape(-1, 1))


N_MAPS = 2 * B_HEADS
PAGE_COLS = PAGE_SIZE * B_HEADS
PAGE_GROUP = 4
HOST_CALLS = 4
PAGE_LOOKAHEAD = 3
K_SLOTS = PAGE_LOOKAHEAD + 1
V_SLOTS = PAGE_LOOKAHEAD + 3
HOSTING_VMEM_LIMIT_BYTES = 56 * 1024 * 1024


def _block_diag_q(q_row):
    col = lax.broadcasted_iota(jnp.int32, (N_MAPS, D_MODEL), 1)
    row = lax.broadcasted_iota(jnp.int32, (N_MAPS, D_MODEL), 0)
    mine = (col // B_DV == row % B_HEADS) & ((col // B_DH) % 2 == row // B_HEADS)
    return jnp.where(mine, q_row * (B_DH ** -0.5), 0.0)


def _swiglu_pages_kernel(pt_ref, x_ref, gpre_ref, gpost_ref, wgu_ref, wd_ref, q_ref, slope_ref, spread_ref,
                         m_in_ref, l_in_ref, acc_in_ref, ck_ref, cv_ref, o_ref, m_ref, l_ref, acc_ref,
                         ffn_acc_ref, kbuf_ref, vbuf_ref, ksem_ref, vsem_ref, s_ref, e_ref, pm_ref, mg_ref,
                         *, first_page, n_groups, n_pages):
    i = pl.program_id(0)
    n_ahead = min(PAGE_LOOKAHEAD, n_groups)

    def copies(g, step=None):
        out = []
        for j in range(PAGE_GROUP):
            page = pt_ref[i if step is None else step, first_page + g * PAGE_GROUP + j]
            ks, vs = g % K_SLOTS, g % V_SLOTS
            out.append(pltpu.make_async_copy(ck_ref.at[page], kbuf_ref.at[ks, j], ksem_ref.at[ks, j]))
            out.append(pltpu.make_async_copy(cv_ref.at[page], vbuf_ref.at[vs, j], vsem_ref.at[vs, j]))
        return out

    def slots_free_after_chunk(g):
        last_v = max([u for u in range(n_groups) if u % V_SLOTS == g % V_SLOTS], default=-3)
        last_k = max([u for u in range(n_groups) if u % K_SLOTS == g % K_SLOTS], default=0)
        return max(last_v + 3, last_k, 0)

    @pl.when(i == 0)
    def _():
        for g in range(n_ahead):
            for cp in copies(g):
                cp.start()

    m_ref[...] = m_in_ref[...]
    l_ref[...] = l_in_ref[...]
    acc_ref[...] = acc_in_ref[...]
    qbd = _block_diag_q(q_ref[0]).astype(BF16)
    row_head = lax.broadcasted_iota(jnp.int32, (N_MAPS, 1), 0) % B_HEADS
    own_head = lax.broadcasted_iota(jnp.int32, (N_MAPS, PAGE_COLS), 1) % B_HEADS == row_head
    lane_pos = lax.broadcasted_iota(jnp.int32, (N_MAPS, PAGE_GROUP * PAGE_SIZE), 1)

    def scores(g):
        for cp in copies(g):
            cp.wait()
        if g + PAGE_LOOKAHEAD < n_groups:
            for cp in copies(g + PAGE_LOOKAHEAD):
                cp.start()
        s = jnp.concatenate([_dot(qbd, kbuf_ref[g % K_SLOTS, j].astype(BF16)) for j in range(PAGE_GROUP)],
                            axis=1)
        kpos = (first_page + g * PAGE_GROUP) * PAGE_SIZE + lane_pos
        s_ref[g % 2] = s - slope_ref[...] * (n_pages * PAGE_SIZE - kpos).astype(F32)

    def softmax_step(g):
        s = s_ref[g % 2]
        m = m_ref[0]
        m_new = jnp.maximum(m, jnp.max(s, axis=-1, keepdims=True))
        alpha = jnp.exp(m - m_new)
        e = jnp.exp(s - m_new[:, :1])
        e_ref[g % 2] = e.astype(BF16)
        l_ref[0] = alpha * l_ref[0] + jnp.sum(e, axis=-1, keepdims=True)
        acc_ref[0] = alpha * acc_ref[0]
        m_ref[0] = m_new
        mg_ref[g % V_SLOTS] = m_new

    def spread_probs(g):
        e = e_ref[g % 2]
        for j in range(PAGE_GROUP):
            spread = _dot(e[:, j * PAGE_SIZE:(j + 1) * PAGE_SIZE], spread_ref[...])
            pm_ref[g % 2, j] = jnp.where(own_head, spread, 0.0).astype(BF16)

    def add_values(g):
        pv = jnp.zeros((N_MAPS, B_DV), F32)
        for j in range(PAGE_GROUP):
            pv = pv + _dot(pm_ref[g % 2, j], vbuf_ref[g % V_SLOTS, j].astype(BF16))
        acc_ref[0] = acc_ref[0] + jnp.exp(mg_ref[g % V_SLOTS] - m_ref[0]) * pv

    def after_chunk(c):
        for stage, g in ((add_values, c - 3), (spread_probs, c - 2), (softmax_step, c - 1), (scores, c)):
            if 0 <= g < n_groups:
                stage(g)
        for g in range(n_ahead):
            if slots_free_after_chunk(g) == c:
                @pl.when(i + 1 < pl.num_programs(0))
                def _():
                    for cp in copies(g, i + 1):
                        cp.start()

    assert all(slots_free_after_chunk(g) < FFN_DIM // FFN_CHUNK for g in range(n_ahead))
    _swiglu_body(x_ref, gpre_ref, gpost_ref, wgu_ref, wd_ref, o_ref, ffn_acc_ref, after_chunk)


def _swiglu_block_hosting_pages(x, g_pre, g_post, w_gu, w_down, which, q, slope_rows, state, ck, cv,
                                page_table, call_index):
    m_rows = x.shape[0]
    tm = _row_tile(m_rows)
    nb, n_pages = page_table.shape
    per_call = n_pages // HOST_CALLS
    n_groups = per_call // PAGE_GROUP
    assert m_rows // tm == nb and per_call * HOST_CALLS == n_pages and n_groups * PAGE_GROUP == per_call
    assert n_groups + 3 <= FFN_DIM // FFN_CHUNK
    tok = lax.broadcasted_iota(jnp.int32, (PAGE_SIZE, PAGE_COLS), 0)
    dst = lax.broadcasted_iota(jnp.int32, (PAGE_SIZE, PAGE_COLS), 1)
    spread = jnp.where(dst // B_HEADS == tok, 1.0, 0.0).astype(BF16)
    row = pl.BlockSpec((tm, D_MODEL), lambda i, pt: (i, 0))
    vec = _resident((1, D_MODEL))
    st = pl.BlockSpec((1, N_MAPS, B_DV), lambda i, pt: (i, 0, 0))
    any_space = pl.BlockSpec(memory_space=pl.ANY)

    def picked(w):
        return pl.BlockSpec((None, None) + w.shape[2:], lambda *_: which + (0, 0), pipeline_mode=pl.Buffered(1))

    kern = functools.partial(_swiglu_pages_kernel, first_page=call_index * per_call, n_groups=n_groups,
                             n_pages=n_pages)
    st_shape = jax.ShapeDtypeStruct((nb, N_MAPS, B_DV), F32)
    out = pl.pallas_call(
        kern,
        grid_spec=pltpu.PrefetchScalarGridSpec(
            num_scalar_prefetch=1,
            grid=(nb,),
            in_specs=[row, vec, vec, picked(w_gu), picked(w_down),
                      pl.BlockSpec((1, 1, D_MODEL), lambda i, pt: (i, 0, 0)),
                      _resident((N_MAPS, 1)), _resident(spread.shape), st, st, st, any_space, any_space],
            out_specs=[row, st, st, st],
            scratch_shapes=[pltpu.VMEM((tm, D_MODEL), F32),
                            pltpu.VMEM((K_SLOTS, PAGE_GROUP, D_MODEL, PAGE_SIZE), F32),
                            pltpu.VMEM((V_SLOTS, PAGE_GROUP, PAGE_COLS, B_DV), F32),
                            pltpu.SemaphoreType.DMA((K_SLOTS, PAGE_GROUP)),
                            pltpu.SemaphoreType.DMA((V_SLOTS, PAGE_GROUP)),
                            pltpu.VMEM((2, N_MAPS, PAGE_GROUP * PAGE_SIZE), F32),
                            pltpu.VMEM((2, N_MAPS, PAGE_GROUP * PAGE_SIZE), BF16),
                            pltpu.VMEM((2, PAGE_GROUP, N_MAPS, PAGE_COLS), BF16),
                            pltpu.VMEM((V_SLOTS, N_MAPS, B_DV), F32)]),
        out_shape=[jax.ShapeDtypeStruct((m_rows, D_MODEL), F32), st_shape, st_shape, st_shape],
        compiler_params=pltpu.CompilerParams(dimension_semantics=("arbitrary",),
                                             vmem_limit_bytes=HOSTING_VMEM_LIMIT_BYTES),
        name="swiglu_block_hosting_pages",
    )(page_table, x, g_pre.reshape(1, -1), g_post.reshape(1, -1), w_gu, w_down, q, slope_rows, spread,
      *state, ck, cv)
    return out[0], tuple(out[1:])


def _attn_finish_kernel(lam_ref, q_ref, kn_ref, vn_ref, sub_ref, m_ref, l_ref, acc_ref, o_ref, *, out_scale):
    s_new = jnp.sum(_block_diag_q(q_ref[0]) * kn_ref[0], axis=-1, keepdims=True)
    m = m_ref[0][:, :1]
    m_new = jnp.maximum(m, s_new)
    alpha = jnp.exp(m - m_new)
    e = jnp.exp(s_new - m_new)
    vn = vn_ref[0]
    l = alpha * l_ref[0] + e
    acc = alpha * acc_ref[0] + e * jnp.concatenate([vn, vn], axis=0)
    o = acc / l
    o = o[:B_HEADS] - lam_ref[0] * o[B_HEADS:]
    o_ref[0] = _rms(o, sub_ref[...]) * out_scale


def _attn_sample_finish(q, k_new, v_new, state, lam, subln, out_scale):
    nb = q.shape[0]
    row = pl.BlockSpec((1, 1, D_MODEL), lambda b: (b, 0, 0))
    per_head = pl.BlockSpec((1, B_HEADS, B_DV), lambda b: (b, 0, 0))
    st = pl.BlockSpec((1, N_MAPS, B_DV), lambda b: (b, 0, 0))
    out = pl.pallas_call(
        functools.partial(_attn_finish_kernel, out_scale=out_scale),
        grid=(nb,),
        in_specs=[pl.BlockSpec(memory_space=pltpu.SMEM), row, row, per_head,
                  pl.BlockSpec((1, B_DV), lambda b: (0, 0)), st, st, st],
        out_specs=per_head,
        out_shape=jax.ShapeDtypeStruct((nb, B_HEADS, B_DV), F32),
        compiler_params=_params("parallel"),
        name="attn_sample_finish",
    )(lam.reshape(1), q, k_new, v_new.reshape(nb, B_HEADS, B_DV), subln.reshape(1, -1), *state)
    return out.reshape(nb, D_MODEL)


def _diff_lambda(lam, layer_idx):
    lam_init = 0.8 - 0.6 * math.exp(-0.3 * layer_idx)
    lam = lam.astype(F32)
    val = jnp.exp(jnp.sum(lam[0] * lam[1])) - jnp.exp(jnp.sum(lam[2] * lam[3])) + lam_init
    return val, lam_init


def _norm_proj_kernel(x_ref, g_ref, *refs, layouts):
    n_w = len(layouts)
    w_refs, o_refs = refs[:n_w], list(refs[n_w:])
    xn = _rms(x_ref[...], g_ref[...]).astype(BF16)
    for w_ref, layout in zip(w_refs, layouts):
        y = _dot(xn, w_ref[...])
        if "n" in layout:
            o_refs.pop(0)[...] = y
        if "t" in layout:
            o_refs.pop(0)[0] = y.T


def _norm_proj(x, g, weights, layouts, nb, t):
    m = x.shape[0]
    tm = _row_tile(t)
    per_seq = t // tm
    nat = (pl.BlockSpec((tm, D_MODEL), lambda i: (i, 0)), jax.ShapeDtypeStruct((m, D_MODEL), F32))
    tra = (pl.BlockSpec((1, D_MODEL, tm), lambda i: (i // per_seq, 0, i % per_seq)),
           jax.ShapeDtypeStruct((nb, D_MODEL, t), F32))
    outs = [{"n": nat, "t": tra}[c] for layout in layouts for c in layout]
    return pl.pallas_call(
        functools.partial(_norm_proj_kernel, layouts=layouts),
        grid=(m // tm,),
        in_specs=[pl.BlockSpec((tm, D_MODEL), lambda i: (i, 0)), _resident((1, D_MODEL))]
                 + [_resident(w.shape) for w in weights],
        out_specs=[o[0] for o in outs],
        out_shape=[o[1] for o in outs],
        compiler_params=_params("parallel"),
        name="norm_proj",
    )(x, g.reshape(1, -1), *weights)


def _alibi_slopes():
    slopes = [2.0 ** (-8.0 * i / B_HEADS) for i in range(1, B_HEADS + 1)]
    assert all(float(np.asarray(s, dtype=BF16)) == s for s in slopes)
    return jnp.asarray(slopes, F32)


ATTN_LAYER = N_A_LAYERS
assert DEPTH == 2


def _trunk_head(x, hgrn_state, project_kv, swiglu, w, *, chunk):
    nb, t, _ = x.shape
    m = nb * t
    x = x.reshape(m, D_MODEL)
    lbs = jnp.cumsum(jax.nn.softmax(w["a_lb"].astype(F32), axis=0), axis=0)
    t_pad = -(-t // chunk) * chunk
    g = w["norm_g"][0]
    x = swiglu(x, g[0], g[1], (0, 0))
    proj = _norm_matmul(x, g[2], w["a_w_in"][0], (A_FDIM, A_FDIM, D_MODEL, D_MODEL))
    seqs = [a.reshape(nb, t, D_MODEL) for a in list(proj) + [x]]
    if t_pad != t:
        seqs = [jnp.pad(a, ((0, 0), (0, t_pad - t), (0, 0))) for a in seqs]
    x, s_new = _hgrn_mixer(*seqs[:4], lbs[0], w["a_gnorm"][0], hgrn_state[0], seqs[4], w["a_w_out"][0], g[3],
                           chunk=chunk, t_valid=t)
    x = x[:, :t].reshape(m, D_MODEL)
    x = swiglu(x, g[4], g[5], (0, 1))
    kv = project_kv(x)
    g = w["norm_g"][ATTN_LAYER]
    x = swiglu(x, g[0], g[1], (ATTN_LAYER, 0))
    return x, s_new[None], kv


def _trunk_tail(x, attn_out, swiglu, w):
    g = w["norm_g"][ATTN_LAYER]
    x = _matmul_norm_res(attn_out.astype(BF16), w["b_w_o"][0], g[3], x)
    return swiglu(x, g[4], g[5], (ATTN_LAYER, 1))


def kernel(x_prompt, x_sample, cache_k, cache_v, state_hgrn, page_table, norm_g, w_ffn_gu, w_ffn_down,
           a_w_in, a_w_out, a_lb, a_gnorm, norm_kv, w_kv, b_w_q, b_w_o, b_lambda, b_subln):
    w = dict(norm_g=norm_g, a_lb=a_lb, a_gnorm=a_gnorm,
             w_ffn_gu=w_ffn_gu.astype(BF16), w_ffn_down=w_ffn_down.astype(BF16),
             a_w_in=a_w_in.astype(BF16), a_w_out=a_w_out.astype(BF16), b_w_o=b_w_o.astype(BF16))
    w_kv, w_q = w_kv.astype(BF16), b_w_q.astype(BF16)[0]
    nb, t, _ = x_prompt.shape
    db, dt, _ = x_sample.shape
    n_phys = cache_k.shape[0]
    slopes = _alibi_slopes()
    lam, lam_init = _diff_lambda(b_lambda[0], ATTN_LAYER)
    out_scale = 1.0 - lam_init
    g_q = norm_g[ATTN_LAYER, 2]

    def plain_swiglu(x, g_pre, g_post, which):
        return _swiglu_block(x, g_pre, g_post, w["w_ffn_gu"], w["w_ffn_down"], which)

    x_s, st_s, (k_s, v_s) = _trunk_head(
        x_sample, state_hgrn, lambda x: _norm_matmul(x, norm_kv, w_kv, (D_MODEL, D_MODEL)), plain_swiglu, w,
        chunk=BF16_SUBLANES)
    (q_s,) = _norm_matmul(x_s, g_q, w_q, (D_MODEL,))
    as_tok = lambda a: a.reshape(db, 1, D_MODEL)

    ck = jnp.transpose(cache_k, (0, 2, 3, 4, 1)).reshape(n_phys, D_MODEL, PAGE_SIZE)
    cv = cache_v.reshape(n_phys, PAGE_COLS, B_DV)
    slope_rows = jnp.tile(slopes, 2).reshape(N_MAPS, 1)
    cache_state = [(jnp.full((db, N_MAPS, B_DV), MASK_VALUE, F32),) + (jnp.zeros((db, N_MAPS, B_DV), F32),) * 2]

    def hosting_swiglu(x, g_pre, g_post, which):
        call_index = len(cache_state) - 1
        y, state = _swiglu_block_hosting_pages(x, g_pre, g_post, w["w_ffn_gu"], w["w_ffn_down"], which,
                                               as_tok(q_s), slope_rows, cache_state[-1], ck, cv, page_table,
                                               call_index)
        cache_state.append(state)
        return y

    s0_prompt = jnp.zeros((N_A_LAYERS, nb, A_HEADS, A_DK, A_DV), state_hgrn.dtype)
    x_p, st_p, (k_nat, kt_p, v_p, vt_p) = _trunk_head(
        x_prompt, s0_prompt,
        lambda x: _norm_proj(x, norm_kv, [w_kv[:, :D_MODEL], w_kv[:, D_MODEL:]], ["nt", "nt"], nb, t),
        hosting_swiglu, w, chunk=32)
    (qt_p,) = _norm_proj(x_p, g_q, [w_q], ["t"], nb, t)
    o_p = _attn_prompt(qt_p, k_nat.reshape(nb, t, D_MODEL), vt_p, slopes, lam, b_subln[0], out_scale)
    y_p = _trunk_tail(x_p, o_p.reshape(nb * t, D_MODEL), hosting_swiglu, w).reshape(nb, t, D_MODEL)
    assert len(cache_state) == HOST_CALLS + 1

    o_s = _attn_sample_finish(as_tok(q_s), as_tok(k_s), as_tok(v_s), cache_state[-1], lam, b_subln[0], out_scale)
    y_s = _trunk_tail(x_s, o_s, plain_swiglu, w).reshape(db, dt, D_MODEL)

    k_p = jnp.transpose(kt_p.reshape(nb, B_HEADS, 2, B_DH, t), (0, 4, 1, 2, 3))
    return (y_p, y_s, st_p,
            k_p, v_p.reshape(nb, t, B_HEADS, B_DV),
            st_s,
            k_s.reshape(db, dt, B_HEADS, 2, B_DH), v_s.reshape(db, dt, B_HEADS, B_DV))
```

```python
import functools
import math

import jax
import jax.numpy as jnp
import numpy as np
from jax import lax
from jax.experimental import pallas as pl
from jax.experimental.pallas import tpu as pltpu

D_MODEL = 1024
DEPTH = 2
N_A_LAYERS = DEPTH // 2
PAGE_SIZE = 128
A_HEADS = 8
A_DK = 128
A_DV = D_MODEL // A_HEADS
A_FDIM = A_HEADS * A_DK
B_HEADS = 8
B_DH = D_MODEL // (2 * B_HEADS)
B_DV = 2 * B_DH
FFN_DIM = 2816
NORM_EPS = 1e-6
MASK_VALUE = -1e30

LANES = 128
BF16_SUBLANES = 16
VMEM_LIMIT_BYTES = 48 * 1024 * 1024

F32 = jnp.float32
BF16 = jnp.bfloat16


def _rms(x, g):
    ms = jnp.mean(x * x, axis=-1, keepdims=True)
    return x * lax.rsqrt(ms + NORM_EPS) * g


def _silu(x):
    return x * jax.nn.sigmoid(x)


def _dot(a, b):
    return jnp.dot(a, b, preferred_element_type=F32)


def _dot_nt(a, b):
    return lax.dot_general(a, b, (((1,), (1,)), ((), ())), preferred_element_type=F32)


def _dot_tn(a, b):
    return lax.dot_general(a, b, (((0,), (0,)), ((), ())), preferred_element_type=F32)


def _resident(shape):
    nd = len(shape)
    return pl.BlockSpec(shape, lambda *_: (0,) * nd, pipeline_mode=pl.Buffered(1))


def _row_tile(m):
    return min(m, 512)


def _params(*sem):
    return pltpu.CompilerParams(dimension_semantics=sem, vmem_limit_bytes=VMEM_LIMIT_BYTES)


FFN_CHUNK = 256


def _swiglu_body(x_ref, gpre_ref, gpost_ref, wgu_ref, wd_ref, o_ref, acc_ref, after_chunk=None):
    x = x_ref[...]
    xn = _rms(x, gpre_ref[...]).astype(BF16)
    for c in range(FFN_DIM // FFN_CHUNK):
        lo = c * FFN_CHUNK
        gate = _dot(xn, wgu_ref[:, lo:lo + FFN_CHUNK])
        up = _dot(xn, wgu_ref[:, FFN_DIM + lo:FFN_DIM + lo + FFN_CHUNK])
        act = (_silu(gate) * up).astype(BF16)
        down = _dot(act, wd_ref[lo:lo + FFN_CHUNK, :])
        if c == 0:
            acc_ref[...] = down
        else:
            acc_ref[...] += down
        if after_chunk is not None:
            after_chunk(c)
    o_ref[...] = x + 0.5 * _rms(acc_ref[...], gpost_ref[...])


def _swiglu_kernel(x_ref, gpre_ref, gpost_ref, wgu_ref, wd_ref, o_ref, acc_ref):
    _swiglu_body(x_ref, gpre_ref, gpost_ref, wgu_ref, wd_ref, o_ref, acc_ref)


def _swiglu_block(x, g_pre, g_post, w_gu, w_down, which):
    m = x.shape[0]
    tm = _row_tile(m)
    row = pl.BlockSpec((tm, D_MODEL), lambda i: (i, 0))
    vec = _resident((1, D_MODEL))

    def picked(w):
        return pl.BlockSpec((None, None) + w.shape[2:], lambda i: which + (0, 0), pipeline_mode=pl.Buffered(1))

    return pl.pallas_call(
        _swiglu_kernel,
        grid=(m // tm,),
        in_specs=[row, vec, vec, picked(w_gu), picked(w_down)],
        out_specs=row,
        out_shape=jax.ShapeDtypeStruct((m, D_MODEL), F32),
        scratch_shapes=[pltpu.VMEM((tm, D_MODEL), F32)],
        compiler_params=_params("parallel"),
        name="swiglu_block",
    )(x, g_pre.reshape(1, -1), g_post.reshape(1, -1), w_gu, w_down)


def _norm_matmul_kernel(x_ref, g_ref, w_ref, *o_refs):
    xn = _rms(x_ref[...], g_ref[...]).astype(BF16)
    off = 0
    for o_ref in o_refs:
        n = o_ref.shape[-1]
        o_ref[...] = _dot(xn, w_ref[:, off:off + n])
        off += n


def _norm_matmul(x, g, w, splits):
    m = x.shape[0]
    tm = _row_tile(m)
    return pl.pallas_call(
        _norm_matmul_kernel,
        grid=(m // tm,),
        in_specs=[pl.BlockSpec((tm, D_MODEL), lambda i: (i, 0)), _resident((1, D_MODEL)), _resident(w.shape)],
        out_specs=[pl.BlockSpec((tm, n), lambda i: (i, 0)) for n in splits],
        out_shape=[jax.ShapeDtypeStruct((m, n), F32) for n in splits],
        compiler_params=_params("parallel"),
        name="norm_matmul",
    )(x, g.reshape(1, -1), w)


def _matmul_norm_res_kernel(a_ref, w_ref, g_ref, x_ref, o_ref):
    o_ref[...] = x_ref[...] + _rms(_dot(a_ref[...], w_ref[...]), g_ref[...])


def _matmul_norm_res(a, w, g, x):
    m = x.shape[0]
    tm = _row_tile(m)
    row = pl.BlockSpec((tm, D_MODEL), lambda i: (i, 0))
    return pl.pallas_call(
        _matmul_norm_res_kernel,
        grid=(m // tm,),
        in_specs=[row, _resident(w.shape), _resident((1, D_MODEL)), row],
        out_specs=row,
        out_shape=jax.ShapeDtypeStruct((m, D_MODEL), F32),
        compiler_params=_params("parallel"),
        name="matmul_norm_res",
    )(a, w, g.reshape(1, -1), x)


SUB = 16
LOG2_E = math.log2(math.e)


def _cumsum_rows(x, tri):
    hi = x.astype(BF16)
    r1 = x - hi.astype(F32)
    mid = r1.astype(BF16)
    lo = (r1 - mid.astype(F32)).astype(BF16)
    return _dot(tri, hi) + _dot(tri, mid) + _dot(tri, lo)


HGRN_TILE = 256


def _hgrn_kernel(q_ref, f_ref, i_ref, g_ref, lb_ref, gn_ref, s0_ref, x_ref, wout_ref, gpost_ref, o_ref, s_out_ref,
                 st_ref, b_ref, b2_ref, k_ref, mix_ref, *, chunk, t_valid):
    tile = q_ref.shape[1]
    n_sub = chunk // SUB
    ti = pl.program_id(1)
    heads = [slice(h * A_DK, (h + 1) * A_DK) for h in range(A_HEADS)]
    lb = lb_ref[...]
    gn = gn_ref[...]
    ri = lax.broadcasted_iota(jnp.int32, (chunk, chunk), 0)
    ci = lax.broadcasted_iota(jnp.int32, (chunk, chunk), 1)
    tri = jnp.where(ri >= ci, 1.0, 0.0).astype(BF16)
    half = SUB // 2
    sub_col = lax.broadcasted_iota(jnp.int32, (half, SUB), 1)
    causal_sub = (lax.broadcasted_iota(jnp.int32, (SUB, SUB), 0) >= lax.broadcasted_iota(jnp.int32, (SUB, SUB), 1))

    @pl.when(ti == 0)
    def _():
        for h in range(A_HEADS):
            st_ref[h] = s0_ref[0, h].T

    def body(c, carry):
        base = pl.multiple_of(c * chunk, chunk)
        rows = pl.ds(base, chunk)
        qq = _silu(q_ref[0, rows, :])
        forget = lb + (1.0 - lb) * jax.nn.sigmoid(f_ref[0, rows, :])
        logf = jnp.log(forget)
        kk = 1.0 - forget
        if t_valid % chunk:
            live = (ti * tile + base + lax.broadcasted_iota(jnp.int32, (chunk, 1), 0)) < t_valid
            qq = jnp.where(live, qq, 0.0)
            logf = jnp.where(live, logf, 0.0)
            kk = jnp.where(live, kk, 0.0)
        b = _cumsum_rows(logf, tri)
        b_ref[...] = b
        b2_ref[...] = b * LOG2_E
        k_ref[...] = kk
        vb = i_ref[0, rows, :].astype(BF16)
        b_last = b_ref[chunk - 1:chunk, :]
        qe = (qq * jnp.exp(b)).astype(BF16)
        kd = (kk * jnp.exp(b_last - b)).astype(BF16)
        decay = jnp.exp(b_last)
        gate = _silu(g_ref[0, rows, :])

        intra = []
        for i in range(n_sub):
            r0 = i * SUB
            qi = qq[r0:r0 + SUB]
            bi = b[r0:r0 + SUB]
            if i == 0:
                acc = [jnp.zeros((SUB, A_DV), F32) for _ in heads]
            else:
                ref_row = b_ref[r0 - 1:r0, :]
                qp = (qi * jnp.exp(bi - ref_row)).astype(BF16)
                kp = (kk[:r0] * jnp.exp(ref_row - b[:r0])).astype(BF16)
                acc = [_dot(_dot_nt(qp[:, hs], kp[:, hs]).astype(BF16), vb[:r0, hs]) for hs in heads]
            top = [jnp.zeros((half, SUB), F32) for _ in heads]
            bot = [jnp.zeros((half, SUB), F32) for _ in heads]
            b2i = bi * LOG2_E
            for s in range(SUB):
                first = 0 if s < half else half
                row_s = slice(r0 + s, r0 + s + 1)
                w = qi[first:] * jnp.exp2(jnp.minimum(b2i[first:] - b2_ref[row_s, :], 0.0)) * k_ref[row_s, :]
                for h, hs in enumerate(heads):
                    a = jnp.sum(w[:, hs], axis=-1, keepdims=True)
                    if s < half:
                        top[h] = jnp.where(sub_col == s, a[:half], top[h])
                    bot[h] = jnp.where(sub_col == s, a[-half:], bot[h])
            for h, hs in enumerate(heads):
                att = jnp.where(causal_sub, jnp.concatenate([top[h], bot[h]], axis=0), 0.0)
                acc[h] = acc[h] + _dot(att.astype(BF16), vb[r0:r0 + SUB, hs])
            intra.append(acc)

        for h, hs in enumerate(heads):
            st = st_ref[h]
            o = _dot_nt(qe[:, hs], st.astype(BF16)) + jnp.concatenate([acc[h] for acc in intra], axis=0)
            st_ref[h] = st * decay[:, hs] + _dot_tn(vb[:, hs], kd[:, hs])
            mix_ref[rows, hs] = (_rms(o, gn) * gate[:, hs]).astype(BF16)
        return carry

    lax.fori_loop(0, tile // chunk, body, 0, unroll=2)
    o_ref[0] = x_ref[0] + _rms(_dot(mix_ref[...], wout_ref[...]), gpost_ref[...])

    @pl.when(ti == pl.num_programs(1) - 1)
    def _():
        for h in range(A_HEADS):
            s_out_ref[0, h] = st_ref[h].T


def _hgrn_mixer(q, f, i, g, lb, gnorm, s0, x, w_out, g_post, *, chunk, t_valid):
    nb, t_pad, _ = q.shape
    tile = min(HGRN_TILE, t_pad)
    seq = pl.BlockSpec((1, tile, D_MODEL), lambda b, t: (b, t, 0))
    state = pl.BlockSpec((1, A_HEADS, A_DK, A_DV), lambda b, t: (b, 0, 0, 0))
    kern = functools.partial(_hgrn_kernel, chunk=chunk, t_valid=t_valid)
    return pl.pallas_call(
        kern,
        grid=(nb, t_pad // tile),
        in_specs=[seq, seq, seq, seq,
                  pl.BlockSpec((1, A_FDIM), lambda b, t: (0, 0)),
                  pl.BlockSpec((1, A_DV), lambda b, t: (0, 0)),
                  state, seq, _resident(w_out.shape), _resident((1, D_MODEL))],
        out_specs=[seq, state],
        out_shape=[jax.ShapeDtypeStruct((nb, t_pad, D_MODEL), F32),
                   jax.ShapeDtypeStruct((nb, A_HEADS, A_DK, A_DV), F32)],
        scratch_shapes=[pltpu.VMEM((A_HEADS, A_DV, A_DK), F32)] + [pltpu.VMEM((chunk, A_FDIM), F32)] * 3
                       + [pltpu.VMEM((tile, D_MODEL), BF16)],
        compiler_params=_params("parallel", "arbitrary"),
        name="hgrn_mixer",
    )(q, f, i, g, lb.reshape(1, -1), gnorm.reshape(1, -1), s0, x, w_out, g_post.reshape(1, -1))


ATTN_TILE = 512
POS_RADIX = 16


def _attn_prompt_kernel(slopes_ref, lam_ref, qt_ref, k_ref, vt_ref, sub_ref, o_ref, ka_ref, vtb_ref,
                        sa_ref, sb_ref, pa_ref, pb_ref, *, out_scale):
    h = pl.program_id(1)
    qi = pl.program_id(2)
    tq = qt_ref.shape[2]
    tk = tq
    n_kv = k_ref.shape[1] // tk

    @pl.when(qi == 0)
    def _():
        lane = lax.broadcasted_iota(jnp.int32, (tk, 2 * B_DH), 1)
        row = lax.broadcasted_iota(jnp.int32, (tk, 2 * B_DH), 0)
        for jt in range(n_kv):
            kpos = jt * tk + row
            digits = jnp.where(lane == B_DH, kpos // POS_RADIX, jnp.where(lane == B_DH + 1, kpos % POS_RADIX,
                               jnp.where(lane < B_DH + 4, 1, 0))).astype(F32)
            kblk = k_ref[0, jt * tk:(jt + 1) * tk, :]
            ka_ref[jt, 0] = jnp.where(lane < B_DH, kblk, digits).astype(BF16)
            ka_ref[jt, 1] = jnp.where(lane < B_DH, pltpu.roll(kblk, B_DH, axis=1), digits).astype(BF16)
            vtb_ref[jt] = vt_ref[0, :, jt * tk:(jt + 1) * tk].astype(BF16)

    slope = slopes_ref[h]
    lam = lam_ref[0]
    qt = qt_ref[0] * (B_DH ** -0.5)
    qpos = qi * tq + lax.broadcasted_iota(jnp.int32, (B_DH, tq), 1)
    r = lax.broadcasted_iota(jnp.int32, (B_DH, tq), 0)
    hi = (qpos // POS_RADIX).astype(F32)
    lo = (qpos % POS_RADIX).astype(F32)
    bias_rows = jnp.where(r == 0, POS_RADIX * slope, jnp.where(r == 1, slope,
                          jnp.where(r == 2, -POS_RADIX * slope * hi, jnp.where(r == 3, -slope * lo, 0.0))))
    qa = [jnp.concatenate([qt[j * B_DH:(j + 1) * B_DH], bias_rows], axis=0).astype(BF16) for j in range(2)]
    causal = (lax.broadcasted_iota(jnp.int32, (tk, tq), 0) <= lax.broadcasted_iota(jnp.int32, (tk, tq), 1))

    s_bufs, p_bufs = (sa_ref, sb_ref), (pa_ref, pb_ref)

    def put_scores(buf, tile, diagonal):
        for j in range(2):
            s = _dot(ka_ref[tile, j], qa[j])
            buf[j] = jnp.where(causal, s, MASK_VALUE) if diagonal else s

    def pipeline_step(slot, stats, accs, next_tile, prev_tile):
        if next_tile is not None:
            put_scores(s_bufs[1 - slot], next_tile, False)
        new_stats, new_accs = [], []
        for j in range(2):
            m, l = stats[j]
            pv = _dot(vtb_ref[prev_tile], p_bufs[1 - slot][j])
            s = s_bufs[slot][j]
            m_new = jnp.maximum(m, jnp.max(s, axis=0, keepdims=True))
            alpha = jnp.exp(m - m_new)
            p = jnp.exp(s - m_new)
            p_bufs[slot][j] = p.astype(BF16)
            new_stats.append((m_new, alpha * l + jnp.sum(p, axis=0, keepdims=True)))
            new_accs.append(alpha * (accs[j] + pv))
        return tuple(new_stats), tuple(new_accs)

    def final_step(slot, stats, accs, prev_tile):
        stats, accs = pipeline_step(slot, stats, accs, None, prev_tile)
        last_tile = jnp.maximum(qi - 1, 0)
        o1, o2 = [(accs[j] + _dot(vtb_ref[last_tile], p_bufs[slot][j])) / stats[j][1] for j in range(2)]
        ot = o1 - lam * o2
        ms = jnp.mean(ot * ot, axis=0, keepdims=True)
        ot = ot * lax.rsqrt(ms + NORM_EPS) * (sub_ref[...] * out_scale)
        o_ref[0] = ot.T.astype(BF16)

    put_scores(sa_ref, qi, True)
    pb_ref[...] = jnp.zeros(pb_ref.shape, BF16)

    def pair(i, carry):
        k = 2 * i
        stats, accs = pipeline_step(0, *carry, next_tile=k, prev_tile=jnp.maximum(k - 2, 0))
        return pipeline_step(1, stats, accs, next_tile=k + 1, prev_tile=jnp.where(k == 0, qi, k - 1))

    one = (jnp.full((1, tq), MASK_VALUE, F32), jnp.zeros((1, tq), F32))
    zero = jnp.zeros((B_DV, tq), F32)
    stats, accs = lax.fori_loop(0, qi // 2, pair, ((one, one), (zero, zero)))

    @pl.when(qi % 2 == 0)
    def _():
        final_step(0, stats, accs, jnp.maximum(qi - 2, 0))

    @pl.when(qi % 2 == 1)
    def _():
        k = qi - 1
        st, ac = pipeline_step(0, stats, accs, next_tile=k, prev_tile=jnp.maximum(k - 2, 0))
        final_step(1, st, ac, jnp.where(k == 0, qi, k - 1))


def _attn_prompt(qt, k, vt, slopes, lam, subln, out_scale):
    nb, t, _ = k.shape
    tq = min(ATTN_TILE, t)
    smem = pl.BlockSpec(memory_space=pltpu.SMEM)
    kern = functools.partial(_attn_prompt_kernel, out_scale=out_scale)
    return pl.pallas_call(
        kern,
        grid=(nb, B_HEADS, t // tq),
        in_specs=[smem, smem,
                  pl.BlockSpec((1, 2 * B_DH, tq), lambda b, h, i: (b, h, i)),
                  pl.BlockSpec((1, t, 2 * B_DH), lambda b, h, i: (b, 0, h)),
                  pl.BlockSpec((1, B_DV, t), lambda b, h, i: (b, h, 0)),
                  pl.BlockSpec((B_DV, 1), lambda b, h, i: (0, 0))],
        out_specs=pl.BlockSpec((1, tq, B_DV), lambda b, h, i: (b, i, h)),
        out_shape=jax.ShapeDtypeStruct((nb, t, D_MODEL), BF16),
        scratch_shapes=[pltpu.VMEM((t // tq, 2, tq, 2 * B_DH), BF16), pltpu.VMEM((t // tq, B_DV, tq), BF16),
                        pltpu.VMEM((2, tq, tq), F32), pltpu.VMEM((2, tq, tq), F32),
                        pltpu.VMEM((2, tq, tq), BF16), pltpu.VMEM((2, tq, tq), BF16)],
        compiler_params=_params("parallel", "parallel", "arbitrary"),
        name="attn_prompt",
    )(slopes, lam.reshape(1), qt, k, vt, subln.reshape(-1, 1))


N_MAPS = 2 * B_HEADS
PAGE_COLS = PAGE_SIZE * B_HEADS
PAGE_GROUP = 4
HOST_CALLS = 4
PAGE_LOOKAHEAD = 3
K_SLOTS = PAGE_LOOKAHEAD + 1
V_SLOTS = PAGE_LOOKAHEAD + 3
HOSTING_VMEM_LIMIT_BYTES = 56 * 1024 * 1024


def _block_diag_q(q_row):
    col = lax.broadcasted_iota(jnp.int32, (N_MAPS, D_MODEL), 1)
    row = lax.broadcasted_iota(jnp.int32, (N_MAPS, D_MODEL), 0)
    mine = (col // B_DV == row % B_HEADS) & ((col // B_DH) % 2 == row // B_HEADS)
    return jnp.where(mine, q_row * (B_DH ** -0.5), 0.0)


def _swiglu_pages_kernel(pt_ref, x_ref, gpre_ref, gpost_ref, wgu_ref, wd_ref, q_ref, slope_ref, spread_ref,
                         m_in_ref, l_in_ref, acc_in_ref, ck_ref, cv_ref, o_ref, m_ref, l_ref, acc_ref,
                         ffn_acc_ref, kbuf_ref, vbuf_ref, ksem_ref, vsem_ref, s_ref, e_ref, pm_ref, mg_ref,
                         *, first_page, n_groups, n_pages):
    i = pl.program_id(0)
    n_ahead = min(PAGE_LOOKAHEAD, n_groups)

    def copies(g, step=None):
        out = []
        for j in range(PAGE_GROUP):
            page = pt_ref[i if step is None else step, first_page + g * PAGE_GROUP + j]
            ks, vs = g % K_SLOTS, g % V_SLOTS
            out.append(pltpu.make_async_copy(ck_ref.at[page], kbuf_ref.at[ks, j], ksem_ref.at[ks, j]))
            out.append(pltpu.make_async_copy(cv_ref.at[page], vbuf_ref.at[vs, j], vsem_ref.at[vs, j]))
        return out

    def slots_free_after_chunk(g):
        last_v = max([u for u in range(n_groups) if u % V_SLOTS == g % V_SLOTS], default=-3)
        last_k = max([u for u in range(n_groups) if u % K_SLOTS == g % K_SLOTS], default=0)
        return max(last_v + 3, last_k, 0)

    @pl.when(i == 0)
    def _():
        for g in range(n_ahead):
            for cp in copies(g):
                cp.start()

    m_ref[...] = m_in_ref[...]
    l_ref[...] = l_in_ref[...]
    acc_ref[...] = acc_in_ref[...]
    qbd = _block_diag_q(q_ref[0]).astype(BF16)
    row_head = lax.broadcasted_iota(jnp.int32, (N_MAPS, 1), 0) % B_HEADS
    own_head = lax.broadcasted_iota(jnp.int32, (N_MAPS, PAGE_COLS), 1) % B_HEADS == row_head
    lane_pos = lax.broadcasted_iota(jnp.int32, (N_MAPS, PAGE_GROUP * PAGE_SIZE), 1)

    def scores(g):
        for cp in copies(g):
            cp.wait()
        if g + PAGE_LOOKAHEAD < n_groups:
            for cp in copies(g + PAGE_LOOKAHEAD):
                cp.start()
        keys = jnp.concatenate([kbuf_ref[g % K_SLOTS, j].astype(BF16) for j in range(PAGE_GROUP)], axis=1)
        s = _dot(qbd, keys)
        kpos = (first_page + g * PAGE_GROUP) * PAGE_SIZE + lane_pos
        s_ref[g % 2] = s - slope_ref[...] * (n_pages * PAGE_SIZE - kpos).astype(F32)

    def softmax_step(g):
        s = s_ref[g % 2]
        m = m_ref[0]
        m_new = jnp.maximum(m, jnp.max(s, axis=-1, keepdims=True))
        alpha = jnp.exp(m - m_new)
        e = jnp.exp(s - m_new[:, :1])
        e_ref[g % 2] = e.astype(BF16)
        l_ref[0] = alpha * l_ref[0] + jnp.sum(e, axis=-1, keepdims=True)
        acc_ref[0] = alpha * acc_ref[0]
        m_ref[0] = m_new
        mg_ref[g % V_SLOTS] = m_new

    def spread_probs(g):
        e = e_ref[g % 2]
        stacked = jnp.concatenate([e[:, j * PAGE_SIZE:(j + 1) * PAGE_SIZE] for j in range(PAGE_GROUP)], axis=0)
        spread = _dot(stacked, spread_ref[...])
        for j in range(PAGE_GROUP):
            pm_ref[g % 2, :, j * PAGE_COLS:(j + 1) * PAGE_COLS] = jnp.where(
                own_head, spread[j * N_MAPS:(j + 1) * N_MAPS], 0.0).astype(BF16)

    def add_values(g):
        values = vbuf_ref[g % V_SLOTS].reshape(PAGE_GROUP * PAGE_COLS, B_DV).astype(BF16)
        pv = _dot(pm_ref[g % 2], values)
        acc_ref[0] = acc_ref[0] + jnp.exp(mg_ref[g % V_SLOTS] - m_ref[0]) * pv

    def after_chunk(c):
        for stage, g in ((add_values, c - 3), (spread_probs, c - 2), (softmax_step, c - 1), (scores, c)):
            if 0 <= g < n_groups:
                stage(g)
        for g in range(n_ahead):
            if slots_free_after_chunk(g) == c:
                @pl.when(i + 1 < pl.num_programs(0))
                def _():
                    for cp in copies(g, i + 1):
                        cp.start()

    assert all(slots_free_after_chunk(g) < FFN_DIM // FFN_CHUNK for g in range(n_ahead))
    _swiglu_body(x_ref, gpre_ref, gpost_ref, wgu_ref, wd_ref, o_ref, ffn_acc_ref, after_chunk)


def _swiglu_block_hosting_pages(x, g_pre, g_post, w_gu, w_down, which, q, slope_rows, state, ck, cv,
                                page_table, call_index):
    m_rows = x.shape[0]
    tm = _row_tile(m_rows)
    nb, n_pages = page_table.shape
    per_call = n_pages // HOST_CALLS
    n_groups = per_call // PAGE_GROUP
    assert m_rows // tm == nb and per_call * HOST_CALLS == n_pages and n_groups * PAGE_GROUP == per_call
    assert n_groups + 3 <= FFN_DIM // FFN_CHUNK
    tok = lax.broadcasted_iota(jnp.int32, (PAGE_SIZE, PAGE_COLS), 0)
    dst = lax.broadcasted_iota(jnp.int32, (PAGE_SIZE, PAGE_COLS), 1)
    spread = jnp.where(dst // B_HEADS == tok, 1.0, 0.0).astype(BF16)
    row = pl.BlockSpec((tm, D_MODEL), lambda i, pt: (i, 0))
    vec = _resident((1, D_MODEL))
    st = pl.BlockSpec((1, N_MAPS, B_DV), lambda i, pt: (i, 0, 0))
    any_space = pl.BlockSpec(memory_space=pl.ANY)

    def picked(w):
        return pl.BlockSpec((None, None) + w.shape[2:], lambda *_: which + (0, 0), pipeline_mode=pl.Buffered(1))

    kern = functools.partial(_swiglu_pages_kernel, first_page=call_index * per_call, n_groups=n_groups,
                             n_pages=n_pages)
    st_shape = jax.ShapeDtypeStruct((nb, N_MAPS, B_DV), F32)
    out = pl.pallas_call(
        kern,
        grid_spec=pltpu.PrefetchScalarGridSpec(
            num_scalar_prefetch=1,
            grid=(nb,),
            in_specs=[row, vec, vec, picked(w_gu), picked(w_down),
                      pl.BlockSpec((1, 1, D_MODEL), lambda i, pt: (i, 0, 0)),
                      _resident((N_MAPS, 1)), _resident(spread.shape), st, st, st, any_space, any_space],
            out_specs=[row, st, st, st],
            scratch_shapes=[pltpu.VMEM((tm, D_MODEL), F32),
                            pltpu.VMEM((K_SLOTS, PAGE_GROUP, D_MODEL, PAGE_SIZE), F32),
                            pltpu.VMEM((V_SLOTS, PAGE_GROUP, PAGE_COLS, B_DV), F32),
                            pltpu.SemaphoreType.DMA((K_SLOTS, PAGE_GROUP)),
                            pltpu.SemaphoreType.DMA((V_SLOTS, PAGE_GROUP)),
                            pltpu.VMEM((2, N_MAPS, PAGE_GROUP * PAGE_SIZE), F32),
                            pltpu.VMEM((2, N_MAPS, PAGE_GROUP * PAGE_SIZE), BF16),
                            pltpu.VMEM((2, N_MAPS, PAGE_GROUP * PAGE_COLS), BF16),
                            pltpu.VMEM((V_SLOTS, N_MAPS, B_DV), F32)]),
        out_shape=[jax.ShapeDtypeStruct((m_rows, D_MODEL), F32), st_shape, st_shape, st_shape],
        compiler_params=pltpu.CompilerParams(dimension_semantics=("arbitrary",),
                                             vmem_limit_bytes=HOSTING_VMEM_LIMIT_BYTES),
        name="swiglu_block_hosting_pages",
    )(page_table, x, g_pre.reshape(1, -1), g_post.reshape(1, -1), w_gu, w_down, q, slope_rows, spread,
      *state, ck, cv)
    return out[0], tuple(out[1:])


def _attn_finish_kernel(lam_ref, q_ref, kn_ref, vn_ref, sub_ref, m_ref, l_ref, acc_ref, o_ref, *, out_scale):
    s_new = jnp.sum(_block_diag_q(q_ref[0]) * kn_ref[0], axis=-1, keepdims=True)
    m = m_ref[0][:, :1]
    m_new = jnp.maximum(m, s_new)
    alpha = jnp.exp(m - m_new)
    e = jnp.exp(s_new - m_new)
    vn = vn_ref[0]
    l = alpha * l_ref[0] + e
    acc = alpha * acc_ref[0] + e * jnp.concatenate([vn, vn], axis=0)
    o = acc / l
    o = o[:B_HEADS] - lam_ref[0] * o[B_HEADS:]
    o_ref[0] = _rms(o, sub_ref[...]) * out_scale


def _attn_sample_finish(q, k_new, v_new, state, lam, subln, out_scale):
    nb = q.shape[0]
    row = pl.BlockSpec((1, 1, D_MODEL), lambda b: (b, 0, 0))
    per_head = pl.BlockSpec((1, B_HEADS, B_DV), lambda b: (b, 0, 0))
    st = pl.BlockSpec((1, N_MAPS, B_DV), lambda b: (b, 0, 0))
    out = pl.pallas_call(
        functools.partial(_attn_finish_kernel, out_scale=out_scale),
        grid=(nb,),
        in_specs=[pl.BlockSpec(memory_space=pltpu.SMEM), row, row, per_head,
                  pl.BlockSpec((1, B_DV), lambda b: (0, 0)), st, st, st],
        out_specs=per_head,
        out_shape=jax.ShapeDtypeStruct((nb, B_HEADS, B_DV), F32),
        compiler_params=_params("parallel"),
        name="attn_sample_finish",
    )(lam.reshape(1), q, k_new, v_new.reshape(nb, B_HEADS, B_DV), subln.reshape(1, -1), *state)
    return out.reshape(nb, D_MODEL)


def _diff_lambda(lam, layer_idx):
    lam_init = 0.8 - 0.6 * math.exp(-0.3 * layer_idx)
    lam = lam.astype(F32)
    val = jnp.exp(jnp.sum(lam[0] * lam[1])) - jnp.exp(jnp.sum(lam[2] * lam[3])) + lam_init
    return val, lam_init


def _norm_proj_kernel(x_ref, g_ref, *refs, layouts):
    n_w = len(layouts)
    w_refs, o_refs = refs[:n_w], list(refs[n_w:])
    xn = _rms(x_ref[...], g_ref[...]).astype(BF16)
    for w_ref, layout in zip(w_refs, layouts):
        y = _dot(xn, w_ref[...])
        if "n" in layout:
            o_refs.pop(0)[...] = y
        if "t" in layout:
            o_refs.pop(0)[0] = y.T


def _norm_proj(x, g, weights, layouts, nb, t):
    m = x.shape[0]
    tm = _row_tile(t)
    per_seq = t // tm
    nat = (pl.BlockSpec((tm, D_MODEL), lambda i: (i, 0)), jax.ShapeDtypeStruct((m, D_MODEL), F32))
    tra = (pl.BlockSpec((1, D_MODEL, tm), lambda i: (i // per_seq, 0, i % per_seq)),
           jax.ShapeDtypeStruct((nb, D_MODEL, t), F32))
    outs = [{"n": nat, "t": tra}[c] for layout in layouts for c in layout]
    return pl.pallas_call(
        functools.partial(_norm_proj_kernel, layouts=layouts),
        grid=(m // tm,),
        in_specs=[pl.BlockSpec((tm, D_MODEL), lambda i: (i, 0)), _resident((1, D_MODEL))]
                 + [_resident(w.shape) for w in weights],
        out_specs=[o[0] for o in outs],
        out_shape=[o[1] for o in outs],
        compiler_params=_params("parallel"),
        name="norm_proj",
    )(x, g.reshape(1, -1), *weights)


def _alibi_slopes():
    slopes = [2.0 ** (-8.0 * i / B_HEADS) for i in range(1, B_HEADS + 1)]
    assert all(float(np.asarray(s, dtype=BF16)) == s for s in slopes)
    return jnp.asarray(slopes, F32)


ATTN_LAYER = N_A_LAYERS
assert DEPTH == 2


def _trunk_head(x, hgrn_state, project_kv, swiglu, w, *, chunk):
    nb, t, _ = x.shape
    m = nb * t
    x = x.reshape(m, D_MODEL)
    lbs = jnp.cumsum(jax.nn.softmax(w["a_lb"].astype(F32), axis=0), axis=0)
    t_pad = -(-t // chunk) * chunk
    g = w["norm_g"][0]
    x = swiglu(x, g[0], g[1], (0, 0))
    proj = _norm_matmul(x, g[2], w["a_w_in"][0], (A_FDIM, A_FDIM, D_MODEL, D_MODEL))
    seqs = [a.reshape(nb, t, D_MODEL) for a in list(proj) + [x]]
    if t_pad != t:
        seqs = [jnp.pad(a, ((0, 0), (0, t_pad - t), (0, 0))) for a in seqs]
    x, s_new = _hgrn_mixer(*seqs[:4], lbs[0], w["a_gnorm"][0], hgrn_state[0], seqs[4], w["a_w_out"][0], g[3],
                           chunk=chunk, t_valid=t)
    x = x[:, :t].reshape(m, D_MODEL)
    x = swiglu(x, g[4], g[5], (0, 1))
    kv = project_kv(x)
    g = w["norm_g"][ATTN_LAYER]
    x = swiglu(x, g[0], g[1], (ATTN_LAYER, 0))
    return x, s_new[None], kv


def _trunk_tail(x, attn_out, swiglu, w):
    g = w["norm_g"][ATTN_LAYER]
    x = _matmul_norm_res(attn_out.astype(BF16), w["b_w_o"][0], g[3], x)
    return swiglu(x, g[4], g[5], (ATTN_LAYER, 1))


def kernel(x_prompt, x_sample, cache_k, cache_v, state_hgrn, page_table, norm_g, w_ffn_gu, w_ffn_down,
           a_w_in, a_w_out, a_lb, a_gnorm, norm_kv, w_kv, b_w_q, b_w_o, b_lambda, b_subln):
    w = dict(norm_g=norm_g, a_lb=a_lb, a_gnorm=a_gnorm,
             w_ffn_gu=w_ffn_gu.astype(BF16), w_ffn_down=w_ffn_down.astype(BF16),
             a_w_in=a_w_in.astype(BF16), a_w_out=a_w_out.astype(BF16), b_w_o=b_w_o.astype(BF16))
    w_kv, w_q = w_kv.astype(BF16), b_w_q.astype(BF16)[0]
    nb, t, _ = x_prompt.shape
    db, dt, _ = x_sample.shape
    n_phys = cache_k.shape[0]
    slopes = _alibi_slopes()
    lam, lam_init = _diff_lambda(b_lambda[0], ATTN_LAYER)
    out_scale = 1.0 - lam_init
    g_q = norm_g[ATTN_LAYER, 2]

    def plain_swiglu(x, g_pre, g_post, which):
        return _swiglu_block(x, g_pre, g_post, w["w_ffn_gu"], w["w_ffn_down"], which)

    x_s, st_s, (k_s, v_s) = _trunk_head(
        x_sample, state_hgrn, lambda x: _norm_matmul(x, norm_kv, w_kv, (D_MODEL, D_MODEL)), plain_swiglu, w,
        chunk=BF16_SUBLANES)
    (q_s,) = _norm_matmul(x_s, g_q, w_q, (D_MODEL,))
    as_tok = lambda a: a.reshape(db, 1, D_MODEL)

    ck = jnp.transpose(cache_k, (0, 2, 3, 4, 1)).reshape(n_phys, D_MODEL, PAGE_SIZE)
    cv = cache_v.reshape(n_phys, PAGE_COLS, B_DV)
    slope_rows = jnp.tile(slopes, 2).reshape(N_MAPS, 1)
    cache_state = [(jnp.full((db, N_MAPS, B_DV), MASK_VALUE, F32),) + (jnp.zeros((db, N_MAPS, B_DV), F32),) * 2]

    def hosting_swiglu(x, g_pre, g_post, which):
        call_index = len(cache_state) - 1
        y, state = _swiglu_block_hosting_pages(x, g_pre, g_post, w["w_ffn_gu"], w["w_ffn_down"], which,
                                               as_tok(q_s), slope_rows, cache_state[-1], ck, cv, page_table,
                                               call_index)
        cache_state.append(state)
        return y

    s0_prompt = jnp.zeros((N_A_LAYERS, nb, A_HEADS, A_DK, A_DV), state_hgrn.dtype)
    x_p, st_p, (k_nat, kt_p, v_p, vt_p) = _trunk_head(
        x_prompt, s0_prompt,
        lambda x: _norm_proj(x, norm_kv, [w_kv[:, :D_MODEL], w_kv[:, D_MODEL:]], ["nt", "nt"], nb, t),
        hosting_swiglu, w, chunk=32)
    (qt_p,) = _norm_proj(x_p, g_q, [w_q], ["t"], nb, t)
    o_p = _attn_prompt(qt_p, k_nat.reshape(nb, t, D_MODEL), vt_p, slopes, lam, b_subln[0], out_scale)
    y_p = _trunk_tail(x_p, o_p.reshape(nb * t, D_MODEL), hosting_swiglu, w).reshape(nb, t, D_MODEL)
    assert len(cache_state) == HOST_CALLS + 1

    o_s = _attn_sample_finish(as_tok(q_s), as_tok(k_s), as_tok(v_s), cache_state[-1], lam, b_subln[0], out_scale)
    y_s = _trunk_tail(x_s, o_s, plain_swiglu, w).reshape(db, dt, D_MODEL)

    k_p = jnp.transpose(kt_p.reshape(nb, B_HEADS, 2, B_DH, t), (0, 4, 1, 2, 3))
    return (y_p, y_s, st_p,
            k_p, v_p.reshape(nb, t, B_HEADS, B_DV),
            st_s,
            k_s.reshape(db, dt, B_HEADS, 2, B_DH), v_s.reshape(db, dt, B_HEADS, B_DV))
```

```python
import functools
import math

import jax
import jax.numpy as jnp
import numpy as np
from jax import lax
from jax.experimental import pallas as pl
from jax.experimental.pallas import tpu as pltpu

D_MODEL = 1024
DEPTH = 2
N_A_LAYERS = DEPTH // 2
PAGE_SIZE = 128
A_HEADS = 8
A_DK = 128
A_DV = D_MODEL // A_HEADS
A_FDIM = A_HEADS * A_DK
B_HEADS = 8
B_DH = D_MODEL // (2 * B_HEADS)
B_DV = 2 * B_DH
FFN_DIM = 2816
NORM_EPS = 1e-6
MASK_VALUE = -1e30

LANES = 128
BF16_SUBLANES = 16
VMEM_LIMIT_BYTES = 48 * 1024 * 1024

F32 = jnp.float32
BF16 = jnp.bfloat16


def _rms(x, g):
    ms = jnp.mean(x * x, axis=-1, keepdims=True)
    return x * lax.rsqrt(ms + NORM_EPS) * g


def _silu(x):
    return x * jax.nn.sigmoid(x)


def _dot(a, b):
    return jnp.dot(a, b, preferred_element_type=F32)


def _dot_nt(a, b):
    return lax.dot_general(a, b, (((1,), (1,)), ((), ())), preferred_element_type=F32)


def _dot_tn(a, b):
    return lax.dot_general(a, b, (((0,), (0,)), ((), ())), preferred_element_type=F32)


def _resident(shape):
    nd = len(shape)
    return pl.BlockSpec(shape, lambda *_: (0,) * nd, pipeline_mode=pl.Buffered(1))


def _row_tile(m):
    return min(m, 512)


def _params(*sem):
    return pltpu.CompilerParams(dimension_semantics=sem, vmem_limit_bytes=VMEM_LIMIT_BYTES)


FFN_CHUNK = 256


def _swiglu_body(x_ref, gpre_ref, gpost_ref, wgu_ref, wd_ref, o_ref, acc_ref, after_chunk=None):
    x = x_ref[...]
    xn = _rms(x, gpre_ref[...]).astype(BF16)
    for c in range(FFN_DIM // FFN_CHUNK):
        lo = c * FFN_CHUNK
        gate = _dot(xn, wgu_ref[:, lo:lo + FFN_CHUNK])
        up = _dot(xn, wgu_ref[:, FFN_DIM + lo:FFN_DIM + lo + FFN_CHUNK])
        act = (_silu(gate) * up).astype(BF16)
        down = _dot(act, wd_ref[lo:lo + FFN_CHUNK, :])
        if c == 0:
            acc_ref[...] = down
        else:
            acc_ref[...] += down
        if after_chunk is not None:
            after_chunk(c)
    o_ref[...] = x + 0.5 * _rms(acc_ref[...], gpost_ref[...])


def _swiglu_kernel(x_ref, gpre_ref, gpost_ref, wgu_ref, wd_ref, o_ref, acc_ref):
    _swiglu_body(x_ref, gpre_ref, gpost_ref, wgu_ref, wd_ref, o_ref, acc_ref)


def _swiglu_block(x, g_pre, g_post, w_gu, w_down, which):
    m = x.shape[0]
    tm = _row_tile(m)
    row = pl.BlockSpec((tm, D_MODEL), lambda i: (i, 0))
    vec = _resident((1, D_MODEL))

    def picked(w):
        return pl.BlockSpec((None, None) + w.shape[2:], lambda i: which + (0, 0), pipeline_mode=pl.Buffered(1))

    return pl.pallas_call(
        _swiglu_kernel,
        grid=(m // tm,),
        in_specs=[row, vec, vec, picked(w_gu), picked(w_down)],
        out_specs=row,
        out_shape=jax.ShapeDtypeStruct((m, D_MODEL), F32),
        scratch_shapes=[pltpu.VMEM((tm, D_MODEL), F32)],
        compiler_params=_params("parallel"),
        name="swiglu_block",
    )(x, g_pre.reshape(1, -1), g_post.reshape(1, -1), w_gu, w_down)


def _norm_matmul_kernel(x_ref, g_ref, w_ref, *o_refs):
    xn = _rms(x_ref[...], g_ref[...]).astype(BF16)
    off = 0
    for o_ref in o_refs:
        n = o_ref.shape[-1]
        o_ref[...] = _dot(xn, w_ref[:, off:off + n])
        off += n


def _norm_matmul(x, g, w, splits):
    m = x.shape[0]
    tm = _row_tile(m)
    return pl.pallas_call(
        _norm_matmul_kernel,
        grid=(m // tm,),
        in_specs=[pl.BlockSpec((tm, D_MODEL), lambda i: (i, 0)), _resident((1, D_MODEL)), _resident(w.shape)],
        out_specs=[pl.BlockSpec((tm, n), lambda i: (i, 0)) for n in splits],
        out_shape=[jax.ShapeDtypeStruct((m, n), F32) for n in splits],
        compiler_params=_params("parallel"),
        name="norm_matmul",
    )(x, g.reshape(1, -1), w)


def _matmul_norm_res_kernel(a_ref, w_ref, g_ref, x_ref, o_ref):
    o_ref[...] = x_ref[...] + _rms(_dot(a_ref[...], w_ref[...]), g_ref[...])


def _matmul_norm_res(a, w, g, x):
    m = x.shape[0]
    tm = _row_tile(m)
    row = pl.BlockSpec((tm, D_MODEL), lambda i: (i, 0))
    return pl.pallas_call(
        _matmul_norm_res_kernel,
        grid=(m // tm,),
        in_specs=[row, _resident(w.shape), _resident((1, D_MODEL)), row],
        out_specs=row,
        out_shape=jax.ShapeDtypeStruct((m, D_MODEL), F32),
        compiler_params=_params("parallel"),
        name="matmul_norm_res",
    )(a, w, g.reshape(1, -1), x)


SUB = 16
LOG2_E = math.log2(math.e)


def _cumsum_rows(x, tri):
    hi = x.astype(BF16)
    r1 = x - hi.astype(F32)
    mid = r1.astype(BF16)
    lo = (r1 - mid.astype(F32)).astype(BF16)
    return _dot(tri, hi) + _dot(tri, mid) + _dot(tri, lo)


HGRN_TILE = 256


def _hgrn_kernel(q_ref, f_ref, i_ref, g_ref, lb_ref, gn_ref, s0_ref, x_ref, wout_ref, gpost_ref, o_ref, s_out_ref,
                 st_ref, b_ref, b2_ref, k_ref, mix_ref, *, chunk, t_valid):
    tile = q_ref.shape[1]
    n_sub = chunk // SUB
    ti = pl.program_id(1)
    heads = [slice(h * A_DK, (h + 1) * A_DK) for h in range(A_HEADS)]
    lb = lb_ref[...]
    gn = gn_ref[...]
    ri = lax.broadcasted_iota(jnp.int32, (chunk, chunk), 0)
    ci = lax.broadcasted_iota(jnp.int32, (chunk, chunk), 1)
    tri = jnp.where(ri >= ci, 1.0, 0.0).astype(BF16)
    half = SUB // 2
    sub_col = lax.broadcasted_iota(jnp.int32, (half, SUB), 1)
    causal_sub = (lax.broadcasted_iota(jnp.int32, (SUB, SUB), 0) >= lax.broadcasted_iota(jnp.int32, (SUB, SUB), 1))

    @pl.when(ti == 0)
    def _():
        for h in range(A_HEADS):
            st_ref[h] = s0_ref[0, h].T

    def body(c, carry):
        base = pl.multiple_of(c * chunk, chunk)
        rows = pl.ds(base, chunk)
        qq = _silu(q_ref[0, rows, :])
        forget = lb + (1.0 - lb) * jax.nn.sigmoid(f_ref[0, rows, :])
        logf = jnp.log(forget)
        kk = 1.0 - forget
        if t_valid % chunk:
            live = (ti * tile + base + lax.broadcasted_iota(jnp.int32, (chunk, 1), 0)) < t_valid
            qq = jnp.where(live, qq, 0.0)
            logf = jnp.where(live, logf, 0.0)
            kk = jnp.where(live, kk, 0.0)
        b = _cumsum_rows(logf, tri)
        b_ref[...] = b
        b2_ref[...] = b * LOG2_E
        k_ref[...] = kk
        vb = i_ref[0, rows, :].astype(BF16)
        b_last = b_ref[chunk - 1:chunk, :]
        qe = (qq * jnp.exp(b)).astype(BF16)
        kd = (kk * jnp.exp(b_last - b)).astype(BF16)
        decay = jnp.exp(b_last)
        gate = _silu(g_ref[0, rows, :])

        intra = []
        for i in range(n_sub):
            r0 = i * SUB
            qi = qq[r0:r0 + SUB]
            bi = b[r0:r0 + SUB]
            if i == 0:
                acc = [jnp.zeros((SUB, A_DV), F32) for _ in heads]
            else:
                ref_row = b_ref[r0 - 1:r0, :]
                qp = (qi * jnp.exp(bi - ref_row)).astype(BF16)
                kp = (kk[:r0] * jnp.exp(ref_row - b[:r0])).astype(BF16)
                acc = [_dot(_dot_nt(qp[:, hs], kp[:, hs]).astype(BF16), vb[:r0, hs]) for hs in heads]
            top = [jnp.zeros((half, SUB), F32) for _ in heads]
            bot = [jnp.zeros((half, SUB), F32) for _ in heads]
            b2i = bi * LOG2_E
            for s in range(SUB):
                first = 0 if s < half else half
                row_s = slice(r0 + s, r0 + s + 1)
                w = qi[first:] * jnp.exp2(jnp.minimum(b2i[first:] - b2_ref[row_s, :], 0.0)) * k_ref[row_s, :]
                for h, hs in enumerate(heads):
                    a = jnp.sum(w[:, hs], axis=-1, keepdims=True)
                    if s < half:
                        top[h] = jnp.where(sub_col == s, a[:half], top[h])
                    bot[h] = jnp.where(sub_col == s, a[-half:], bot[h])
            for h, hs in enumerate(heads):
                att = jnp.where(causal_sub, jnp.concatenate([top[h], bot[h]], axis=0), 0.0)
                acc[h] = acc[h] + _dot(att.astype(BF16), vb[r0:r0 + SUB, hs])
            intra.append(acc)

        for h, hs in enumerate(heads):
            st = st_ref[h]
            o = _dot_nt(qe[:, hs], st.astype(BF16)) + jnp.concatenate([acc[h] for acc in intra], axis=0)
            st_ref[h] = st * decay[:, hs] + _dot_tn(vb[:, hs], kd[:, hs])
            mix_ref[rows, hs] = (_rms(o, gn) * gate[:, hs]).astype(BF16)
        return carry

    lax.fori_loop(0, tile // chunk, body, 0, unroll=2)
    o_ref[0] = x_ref[0] + _rms(_dot(mix_ref[...], wout_ref[...]), gpost_ref[...])

    @pl.when(ti == pl.num_programs(1) - 1)
    def _():
        for h in range(A_HEADS):
            s_out_ref[0, h] = st_ref[h].T


def _hgrn_mixer(q, f, i, g, lb, gnorm, s0, x, w_out, g_post, *, chunk, t_valid):
    nb, t_pad, _ = q.shape
    tile = min(HGRN_TILE, t_pad)
    seq = pl.BlockSpec((1, tile, D_MODEL), lambda b, t: (b, t, 0))
    state = pl.BlockSpec((1, A_HEADS, A_DK, A_DV), lambda b, t: (b, 0, 0, 0))
    kern = functools.partial(_hgrn_kernel, chunk=chunk, t_valid=t_valid)
    return pl.pallas_call(
        kern,
        grid=(nb, t_pad // tile),
        in_specs=[seq, seq, seq, seq,
                  pl.BlockSpec((1, A_FDIM), lambda b, t: (0, 0)),
                  pl.BlockSpec((1, A_DV), lambda b, t: (0, 0)),
                  state, seq, _resident(w_out.shape), _resident((1, D_MODEL))],
        out_specs=[seq, state],
        out_shape=[jax.ShapeDtypeStruct((nb, t_pad, D_MODEL), F32),
                   jax.ShapeDtypeStruct((nb, A_HEADS, A_DK, A_DV), F32)],
        scratch_shapes=[pltpu.VMEM((A_HEADS, A_DV, A_DK), F32)] + [pltpu.VMEM((chunk, A_FDIM), F32)] * 3
                       + [pltpu.VMEM((tile, D_MODEL), BF16)],
        compiler_params=_params("parallel", "arbitrary"),
        name="hgrn_mixer",
    )(q, f, i, g, lb.reshape(1, -1), gnorm.reshape(1, -1), s0, x, w_out, g_post.reshape(1, -1))


ATTN_TILE = 512
POS_RADIX = 16


def _attn_prompt_kernel(slopes_ref, lam_ref, qt_ref, k_ref, vt_ref, sub_ref, o_ref, ka_ref, vtb_ref,
                        sa_ref, sb_ref, pa_ref, pb_ref, *, out_scale):
    h = pl.program_id(1)
    t = k_ref.shape[1]
    tq = tk = min(ATTN_TILE, t)
    n_kv = t // tk

    lane = lax.broadcasted_iota(jnp.int32, (tk, 2 * B_DH), 1)
    row = lax.broadcasted_iota(jnp.int32, (tk, 2 * B_DH), 0)
    for jt in range(n_kv):
        kpos = jt * tk + row
        digits = jnp.where(lane == B_DH, kpos // POS_RADIX, jnp.where(lane == B_DH + 1, kpos % POS_RADIX,
                           jnp.where(lane < B_DH + 4, 1, 0))).astype(F32)
        kblk = k_ref[0, jt * tk:(jt + 1) * tk, :]
        ka_ref[jt, 0] = jnp.where(lane < B_DH, kblk, digits).astype(BF16)
        ka_ref[jt, 1] = jnp.where(lane < B_DH, pltpu.roll(kblk, B_DH, axis=1), digits).astype(BF16)
        vtb_ref[jt] = vt_ref[0, :, jt * tk:(jt + 1) * tk].astype(BF16)

    slope = slopes_ref[h]
    lam = lam_ref[0]
    r = lax.broadcasted_iota(jnp.int32, (B_DH, tq), 0)
    causal = (lax.broadcasted_iota(jnp.int32, (tk, tq), 0) <= lax.broadcasted_iota(jnp.int32, (tk, tq), 1))
    s_bufs, p_bufs = (sa_ref, sb_ref), (pa_ref, pb_ref)
    one = (jnp.full((1, tq), MASK_VALUE, F32), jnp.zeros((1, tq), F32))
    zero = jnp.zeros((B_DV, tq), F32)

    def query_tile(qi):
        qt = qt_ref[0, :, qi * tq:(qi + 1) * tq] * (B_DH ** -0.5)
        qpos = qi * tq + lax.broadcasted_iota(jnp.int32, (B_DH, tq), 1)
        hi = (qpos // POS_RADIX).astype(F32)
        lo = (qpos % POS_RADIX).astype(F32)
        bias_rows = jnp.where(r == 0, POS_RADIX * slope, jnp.where(r == 1, slope,
                              jnp.where(r == 2, -POS_RADIX * slope * hi, jnp.where(r == 3, -slope * lo, 0.0))))
        qa = [jnp.concatenate([qt[j * B_DH:(j + 1) * B_DH], bias_rows], axis=0).astype(BF16) for j in range(2)]

        def put_scores(buf, tile, diagonal):
            for j in range(2):
                s = _dot(ka_ref[tile, j], qa[j])
                buf[j] = jnp.where(causal, s, MASK_VALUE) if diagonal else s

        def pipeline_step(slot, stats, accs, next_tile, prev_tile):
            if next_tile is not None:
                put_scores(s_bufs[1 - slot], next_tile, False)
            new_stats, new_accs = [], []
            for j in range(2):
                m, l = stats[j]
                pv = _dot(vtb_ref[prev_tile], p_bufs[1 - slot][j])
                s = s_bufs[slot][j]
                m_new = jnp.maximum(m, jnp.max(s, axis=0, keepdims=True))
                alpha = jnp.exp(m - m_new)
                p = jnp.exp(s - m_new)
                p_bufs[slot][j] = p.astype(BF16)
                new_stats.append((m_new, alpha * l + jnp.sum(p, axis=0, keepdims=True)))
                new_accs.append(alpha * (accs[j] + pv))
            return tuple(new_stats), tuple(new_accs)

        def final_step(slot, stats, accs, prev_tile):
            stats, accs = pipeline_step(slot, stats, accs, None, prev_tile)
            last_tile = max(qi - 1, 0)
            o1, o2 = [(accs[j] + _dot(vtb_ref[last_tile], p_bufs[slot][j])) / stats[j][1] for j in range(2)]
            ot = o1 - lam * o2
            ms = jnp.mean(ot * ot, axis=0, keepdims=True)
            ot = ot * lax.rsqrt(ms + NORM_EPS) * (sub_ref[...] * out_scale)
            o_ref[0, qi * tq:(qi + 1) * tq, :] = ot.T.astype(BF16)

        put_scores(sa_ref, qi, True)
        pb_ref[...] = jnp.zeros(pb_ref.shape, BF16)
        stats, accs = (one, one), (zero, zero)
        for k in range(0, qi - 1, 2):
            stats, accs = pipeline_step(0, stats, accs, next_tile=k, prev_tile=max(k - 2, 0))
            stats, accs = pipeline_step(1, stats, accs, next_tile=k + 1, prev_tile=qi if k == 0 else k - 1)
        if qi % 2 == 0:
            final_step(0, stats, accs, max(qi - 2, 0))
        else:
            k = qi - 1
            stats, accs = pipeline_step(0, stats, accs, next_tile=k, prev_tile=max(k - 2, 0))
            final_step(1, stats, accs, qi if k == 0 else k - 1)

    for qi in range(n_kv):
        query_tile(qi)


def _attn_prompt(qt, k, vt, slopes, lam, subln, out_scale):
    nb, t, _ = k.shape
    tq = min(ATTN_TILE, t)
    smem = pl.BlockSpec(memory_space=pltpu.SMEM)
    kern = functools.partial(_attn_prompt_kernel, out_scale=out_scale)
    return pl.pallas_call(
        kern,
        grid=(nb, B_HEADS),
        in_specs=[smem, smem,
                  pl.BlockSpec((1, 2 * B_DH, t), lambda b, h: (b, h, 0)),
                  pl.BlockSpec((1, t, 2 * B_DH), lambda b, h: (b, 0, h)),
                  pl.BlockSpec((1, B_DV, t), lambda b, h: (b, h, 0)),
                  pl.BlockSpec((B_DV, 1), lambda b, h: (0, 0))],
        out_specs=pl.BlockSpec((1, t, B_DV), lambda b, h: (b, 0, h)),
        out_shape=jax.ShapeDtypeStruct((nb, t, D_MODEL), BF16),
        scratch_shapes=[pltpu.VMEM((t // tq, 2, tq, 2 * B_DH), BF16), pltpu.VMEM((t // tq, B_DV, tq), BF16),
                        pltpu.VMEM((2, tq, tq), F32), pltpu.VMEM((2, tq, tq), F32),
                        pltpu.VMEM((2, tq, tq), BF16), pltpu.VMEM((2, tq, tq), BF16)],
        compiler_params=_params("parallel", "parallel"),
        name="attn_prompt",
    )(slopes, lam.reshape(1), qt, k, vt, subln.reshape(-1, 1))


N_MAPS = 2 * B_HEADS
PAGE_COLS = PAGE_SIZE * B_HEADS
PAGE_GROUP = 4
HOST_CALLS = 4
PAGE_LOOKAHEAD = 3
K_SLOTS = PAGE_LOOKAHEAD + 1
V_SLOTS = PAGE_LOOKAHEAD + 3
HOSTING_VMEM_LIMIT_BYTES = 56 * 1024 * 1024


def _block_diag_q(q_row):
    col = lax.broadcasted_iota(jnp.int32, (N_MAPS, D_MODEL), 1)
    row = lax.broadcasted_iota(jnp.int32, (N_MAPS, D_MODEL), 0)
    mine = (col // B_DV == row % B_HEADS) & ((col // B_DH) % 2 == row // B_HEADS)
    return jnp.where(mine, q_row * (B_DH ** -0.5), 0.0)


def _swiglu_pages_kernel(pt_ref, x_ref, gpre_ref, gpost_ref, wgu_ref, wd_ref, q_ref, slope_ref, spread_ref,
                         m_in_ref, l_in_ref, acc_in_ref, ck_ref, cv_ref, o_ref, m_ref, l_ref, acc_ref,
                         ffn_acc_ref, kbuf_ref, vbuf_ref, ksem_ref, vsem_ref, s_ref, e_ref, pm_ref, mg_ref,
                         *, first_page, n_groups, n_pages):
    i = pl.program_id(0)
    n_ahead = min(PAGE_LOOKAHEAD, n_groups)

    def copies(g, step=None):
        out = []
        for j in range(PAGE_GROUP):
            page = pt_ref[i if step is None else step, first_page + g * PAGE_GROUP + j]
            ks, vs = g % K_SLOTS, g % V_SLOTS
            out.append(pltpu.make_async_copy(ck_ref.at[page], kbuf_ref.at[ks, j], ksem_ref.at[ks, j]))
            out.append(pltpu.make_async_copy(cv_ref.at[page], vbuf_ref.at[vs, j], vsem_ref.at[vs, j]))
        return out

    def slots_free_after_chunk(g):
        last_v = max([u for u in range(n_groups) if u % V_SLOTS == g % V_SLOTS], default=-3)
        last_k = max([u for u in range(n_groups) if u % K_SLOTS == g % K_SLOTS], default=0)
        return max(last_v + 3, last_k, 0)

    @pl.when(i == 0)
    def _():
        for g in range(n_ahead):
            for cp in copies(g):
                cp.start()

    m_ref[...] = m_in_ref[...]
    l_ref[...] = l_in_ref[...]
    acc_ref[...] = acc_in_ref[...]
    qbd = _block_diag_q(q_ref[0]).astype(BF16)
    row_head = lax.broadcasted_iota(jnp.int32, (N_MAPS, 1), 0) % B_HEADS
    own_head = lax.broadcasted_iota(jnp.int32, (N_MAPS, PAGE_COLS), 1) % B_HEADS == row_head
    lane_pos = lax.broadcasted_iota(jnp.int32, (N_MAPS, PAGE_GROUP * PAGE_SIZE), 1)

    def scores(g):
        for cp in copies(g):
            cp.wait()
        if g + PAGE_LOOKAHEAD < n_groups:
            for cp in copies(g + PAGE_LOOKAHEAD):
                cp.start()
        keys = jnp.concatenate([kbuf_ref[g % K_SLOTS, j].astype(BF16) for j in range(PAGE_GROUP)], axis=1)
        s = _dot(qbd, keys)
        kpos = (first_page + g * PAGE_GROUP) * PAGE_SIZE + lane_pos
        s_ref[g % 2] = s - slope_ref[...] * (n_pages * PAGE_SIZE - kpos).astype(F32)

    def softmax_step(g):
        s = s_ref[g % 2]
        m = m_ref[0]
        m_new = jnp.maximum(m, jnp.max(s, axis=-1, keepdims=True))
        alpha = jnp.exp(m - m_new)
        e = jnp.exp(s - m_new[:, :1])
        e_ref[g % 2] = e.astype(BF16)
        l_ref[0] = alpha * l_ref[0] + jnp.sum(e, axis=-1, keepdims=True)
        acc_ref[0] = alpha * acc_ref[0]
        m_ref[0] = m_new
        mg_ref[g % V_SLOTS] = m_new

    def spread_probs(g):
        e = e_ref[g % 2]
        stacked = jnp.concatenate([e[:, j * PAGE_SIZE:(j + 1) * PAGE_SIZE] for j in range(PAGE_GROUP)], axis=0)
        spread = _dot(stacked, spread_ref[...])
        for j in range(PAGE_GROUP):
            pm_ref[g % 2, :, j * PAGE_COLS:(j + 1) * PAGE_COLS] = jnp.where(
                own_head, spread[j * N_MAPS:(j + 1) * N_MAPS], 0.0).astype(BF16)

    def add_values(g):
        values = vbuf_ref[g % V_SLOTS].reshape(PAGE_GROUP * PAGE_COLS, B_DV).astype(BF16)
        pv = _dot(pm_ref[g % 2], values)
        acc_ref[0] = acc_ref[0] + jnp.exp(mg_ref[g % V_SLOTS] - m_ref[0]) * pv

    def after_chunk(c):
        for stage, g in ((add_values, c - 3), (spread_probs, c - 2), (softmax_step, c - 1), (scores, c)):
            if 0 <= g < n_groups:
                stage(g)
        for g in range(n_ahead):
            if slots_free_after_chunk(g) == c:
                @pl.when(i + 1 < pl.num_programs(0))
                def _():
                    for cp in copies(g, i + 1):
                        cp.start()

    assert all(slots_free_after_chunk(g) < FFN_DIM // FFN_CHUNK for g in range(n_ahead))
    _swiglu_body(x_ref, gpre_ref, gpost_ref, wgu_ref, wd_ref, o_ref, ffn_acc_ref, after_chunk)


def _swiglu_block_hosting_pages(x, g_pre, g_post, w_gu, w_down, which, q, slope_rows, state, ck, cv,
                                page_table, call_index):
    m_rows = x.shape[0]
    tm = _row_tile(m_rows)
    nb, n_pages = page_table.shape
    per_call = n_pages // HOST_CALLS
    n_groups = per_call // PAGE_GROUP
    assert m_rows // tm == nb and per_call * HOST_CALLS == n_pages and n_groups * PAGE_GROUP == per_call
    assert n_groups + 3 <= FFN_DIM // FFN_CHUNK
    tok = lax.broadcasted_iota(jnp.int32, (PAGE_SIZE, PAGE_COLS), 0)
    dst = lax.broadcasted_iota(jnp.int32, (PAGE_SIZE, PAGE_COLS), 1)
    spread = jnp.where(dst // B_HEADS == tok, 1.0, 0.0).astype(BF16)
    row = pl.BlockSpec((tm, D_MODEL), lambda i, pt: (i, 0))
    vec = _resident((1, D_MODEL))
    st = pl.BlockSpec((1, N_MAPS, B_DV), lambda i, pt: (i, 0, 0))
    any_space = pl.BlockSpec(memory_space=pl.ANY)

    def picked(w):
        return pl.BlockSpec((None, None) + w.shape[2:], lambda *_: which + (0, 0), pipeline_mode=pl.Buffered(1))

    kern = functools.partial(_swiglu_pages_kernel, first_page=call_index * per_call, n_groups=n_groups,
                             n_pages=n_pages)
    st_shape = jax.ShapeDtypeStruct((nb, N_MAPS, B_DV), F32)
    out = pl.pallas_call(
        kern,
        grid_spec=pltpu.PrefetchScalarGridSpec(
            num_scalar_prefetch=1,
            grid=(nb,),
            in_specs=[row, vec, vec, picked(w_gu), picked(w_down),
                      pl.BlockSpec((1, 1, D_MODEL), lambda i, pt: (i, 0, 0)),
                      _resident((N_MAPS, 1)), _resident(spread.shape), st, st, st, any_space, any_space],
            out_specs=[row, st, st, st],
            scratch_shapes=[pltpu.VMEM((tm, D_MODEL), F32),
                            pltpu.VMEM((K_SLOTS, PAGE_GROUP, D_MODEL, PAGE_SIZE), F32),
                            pltpu.VMEM((V_SLOTS, PAGE_GROUP, PAGE_COLS, B_DV), F32),
                            pltpu.SemaphoreType.DMA((K_SLOTS, PAGE_GROUP)),
                            pltpu.SemaphoreType.DMA((V_SLOTS, PAGE_GROUP)),
                            pltpu.VMEM((2, N_MAPS, PAGE_GROUP * PAGE_SIZE), F32),
                            pltpu.VMEM((2, N_MAPS, PAGE_GROUP * PAGE_SIZE), BF16),
                            pltpu.VMEM((2, N_MAPS, PAGE_GROUP * PAGE_COLS), BF16),
                            pltpu.VMEM((V_SLOTS, N_MAPS, B_DV), F32)]),
        out_shape=[jax.ShapeDtypeStruct((m_rows, D_MODEL), F32), st_shape, st_shape, st_shape],
        compiler_params=pltpu.CompilerParams(dimension_semantics=("arbitrary",),
                                             vmem_limit_bytes=HOSTING_VMEM_LIMIT_BYTES),
        name="swiglu_block_hosting_pages",
    )(page_table, x, g_pre.reshape(1, -1), g_post.reshape(1, -1), w_gu, w_down, q, slope_rows, spread,
      *state, ck, cv)
    return out[0], tuple(out[1:])


def _attn_finish_kernel(lam_ref, q_ref, kn_ref, vn_ref, sub_ref, m_ref, l_ref, acc_ref, o_ref, *, out_scale):
    s_new = jnp.sum(_block_diag_q(q_ref[0]) * kn_ref[0], axis=-1, keepdims=True)
    m = m_ref[0][:, :1]
    m_new = jnp.maximum(m, s_new)
    alpha = jnp.exp(m - m_new)
    e = jnp.exp(s_new - m_new)
    vn = vn_ref[0]
    l = alpha * l_ref[0] + e
    acc = alpha * acc_ref[0] + e * jnp.concatenate([vn, vn], axis=0)
    o = acc / l
    o = o[:B_HEADS] - lam_ref[0] * o[B_HEADS:]
    o_ref[0] = _rms(o, sub_ref[...]) * out_scale


def _attn_sample_finish(q, k_new, v_new, state, lam, subln, out_scale):
    nb = q.shape[0]
    row = pl.BlockSpec((1, 1, D_MODEL), lambda b: (b, 0, 0))
    per_head = pl.BlockSpec((1, B_HEADS, B_DV), lambda b: (b, 0, 0))
    st = pl.BlockSpec((1, N_MAPS, B_DV), lambda b: (b, 0, 0))
    out = pl.pallas_call(
        functools.partial(_attn_finish_kernel, out_scale=out_scale),
        grid=(nb,),
        in_specs=[pl.BlockSpec(memory_space=pltpu.SMEM), row, row, per_head,
                  pl.BlockSpec((1, B_DV), lambda b: (0, 0)), st, st, st],
        out_specs=per_head,
        out_shape=jax.ShapeDtypeStruct((nb, B_HEADS, B_DV), F32),
        compiler_params=_params("parallel"),
        name="attn_sample_finish",
    )(lam.reshape(1), q, k_new, v_new.reshape(nb, B_HEADS, B_DV), subln.reshape(1, -1), *state)
    return out.reshape(nb, D_MODEL)


def _diff_lambda(lam, layer_idx):
    lam_init = 0.8 - 0.6 * math.exp(-0.3 * layer_idx)
    lam = lam.astype(F32)
    val = jnp.exp(jnp.sum(lam[0] * lam[1])) - jnp.exp(jnp.sum(lam[2] * lam[3])) + lam_init
    return val, lam_init


def _norm_proj_kernel(x_ref, g_ref, *refs, layouts):
    n_w = len(layouts)
    w_refs, o_refs = refs[:n_w], list(refs[n_w:])
    xn = _rms(x_ref[...], g_ref[...]).astype(BF16)
    for w_ref, layout in zip(w_refs, layouts):
        y = _dot(xn, w_ref[...])
        if "n" in layout:
            o_refs.pop(0)[...] = y
        if "t" in layout:
            o_refs.pop(0)[0] = y.T


def _norm_proj(x, g, weights, layouts, nb, t):
    m = x.shape[0]
    tm = _row_tile(t)
    per_seq = t // tm
    nat = (pl.BlockSpec((tm, D_MODEL), lambda i: (i, 0)), jax.ShapeDtypeStruct((m, D_MODEL), F32))
    tra = (pl.BlockSpec((1, D_MODEL, tm), lambda i: (i // per_seq, 0, i % per_seq)),
           jax.ShapeDtypeStruct((nb, D_MODEL, t), F32))
    outs = [{"n": nat, "t": tra}[c] for layout in layouts for c in layout]
    return pl.pallas_call(
        functools.partial(_norm_proj_kernel, layouts=layouts),
        grid=(m // tm,),
        in_specs=[pl.BlockSpec((tm, D_MODEL), lambda i: (i, 0)), _resident((1, D_MODEL))]
                 + [_resident(w.shape) for w in weights],
        out_specs=[o[0] for o in outs],
        out_shape=[o[1] for o in outs],
        compiler_params=_params("parallel"),
        name="norm_proj",
    )(x, g.reshape(1, -1), *weights)


def _alibi_slopes():
    slopes = [2.0 ** (-8.0 * i / B_HEADS) for i in range(1, B_HEADS + 1)]
    assert all(float(np.asarray(s, dtype=BF16)) == s for s in slopes)
    return jnp.asarray(slopes, F32)


ATTN_LAYER = N_A_LAYERS
assert DEPTH == 2


def _trunk_head(x, hgrn_state, project_kv, swiglu, w, *, chunk):
    nb, t, _ = x.shape
    m = nb * t
    x = x.reshape(m, D_MODEL)
    lbs = jnp.cumsum(jax.nn.softmax(w["a_lb"].astype(F32), axis=0), axis=0)
    t_pad = -(-t // chunk) * chunk
    g = w["norm_g"][0]
    x = swiglu(x, g[0], g[1], (0, 0))
    proj = _norm_matmul(x, g[2], w["a_w_in"][0], (A_FDIM, A_FDIM, D_MODEL, D_MODEL))
    seqs = [a.reshape(nb, t, D_MODEL) for a in list(proj) + [x]]
    if t_pad != t:
        seqs = [jnp.pad(a, ((0, 0), (0, t_pad - t), (0, 0))) for a in seqs]
    x, s_new = _hgrn_mixer(*seqs[:4], lbs[0], w["a_gnorm"][0], hgrn_state[0], seqs[4], w["a_w_out"][0], g[3],
                           chunk=chunk, t_valid=t)
    x = x[:, :t].reshape(m, D_MODEL)
    x = swiglu(x, g[4], g[5], (0, 1))
    kv = project_kv(x)
    g = w["norm_g"][ATTN_LAYER]
    x = swiglu(x, g[0], g[1], (ATTN_LAYER, 0))
    return x, s_new[None], kv


def _trunk_tail(x, attn_out, swiglu, w):
    g = w["norm_g"][ATTN_LAYER]
    x = _matmul_norm_res(attn_out.astype(BF16), w["b_w_o"][0], g[3], x)
    return swiglu(x, g[4], g[5], (ATTN_LAYER, 1))


def kernel(x_prompt, x_sample, cache_k, cache_v, state_hgrn, page_table, norm_g, w_ffn_gu, w_ffn_down,
           a_w_in, a_w_out, a_lb, a_gnorm, norm_kv, w_kv, b_w_q, b_w_o, b_lambda, b_subln):
    w = dict(norm_g=norm_g, a_lb=a_lb, a_gnorm=a_gnorm,
             w_ffn_gu=w_ffn_gu.astype(BF16), w_ffn_down=w_ffn_down.astype(BF16),
             a_w_in=a_w_in.astype(BF16), a_w_out=a_w_out.astype(BF16), b_w_o=b_w_o.astype(BF16))
    w_kv, w_q = w_kv.astype(BF16), b_w_q.astype(BF16)[0]
    nb, t, _ = x_prompt.shape
    db, dt, _ = x_sample.shape
    n_phys = cache_k.shape[0]
    slopes = _alibi_slopes()
    lam, lam_init = _diff_lambda(b_lambda[0], ATTN_LAYER)
    out_scale = 1.0 - lam_init
    g_q = norm_g[ATTN_LAYER, 2]

    def plain_swiglu(x, g_pre, g_post, which):
        return _swiglu_block(x, g_pre, g_post, w["w_ffn_gu"], w["w_ffn_down"], which)

    x_s, st_s, (k_s, v_s) = _trunk_head(
        x_sample, state_hgrn, lambda x: _norm_matmul(x, norm_kv, w_kv, (D_MODEL, D_MODEL)), plain_swiglu, w,
        chunk=BF16_SUBLANES)
    (q_s,) = _norm_matmul(x_s, g_q, w_q, (D_MODEL,))
    as_tok = lambda a: a.reshape(db, 1, D_MODEL)

    ck = jnp.transpose(cache_k, (0, 2, 3, 4, 1)).reshape(n_phys, D_MODEL, PAGE_SIZE)
    cv = cache_v.reshape(n_phys, PAGE_COLS, B_DV)
    slope_rows = jnp.tile(slopes, 2).reshape(N_MAPS, 1)
    cache_state = [(jnp.full((db, N_MAPS, B_DV), MASK_VALUE, F32),) + (jnp.zeros((db, N_MAPS, B_DV), F32),) * 2]

    def hosting_swiglu(x, g_pre, g_post, which):
        call_index = len(cache_state) - 1
        y, state = _swiglu_block_hosting_pages(x, g_pre, g_post, w["w_ffn_gu"], w["w_ffn_down"], which,
                                               as_tok(q_s), slope_rows, cache_state[-1], ck, cv, page_table,
                                               call_index)
        cache_state.append(state)
        return y

    s0_prompt = jnp.zeros((N_A_LAYERS, nb, A_HEADS, A_DK, A_DV), state_hgrn.dtype)
    x_p, st_p, (k_nat, kt_p, v_p, vt_p) = _trunk_head(
        x_prompt, s0_prompt,
        lambda x: _norm_proj(x, norm_kv, [w_kv[:, :D_MODEL], w_kv[:, D_MODEL:]], ["nt", "nt"], nb, t),
        hosting_swiglu, w, chunk=32)
    (qt_p,) = _norm_proj(x_p, g_q, [w_q], ["t"], nb, t)
    o_p = _attn_prompt(qt_p, k_nat.reshape(nb, t, D_MODEL), vt_p, slopes, lam, b_subln[0], out_scale)
    y_p = _trunk_tail(x_p, o_p.reshape(nb * t, D_MODEL), hosting_swiglu, w).reshape(nb, t, D_MODEL)
    assert len(cache_state) == HOST_CALLS + 1

    o_s = _attn_sample_finish(as_tok(q_s), as_tok(k_s), as_tok(v_s), cache_state[-1], lam, b_subln[0], out_scale)
    y_s = _trunk_tail(x_s, o_s, plain_swiglu, w).reshape(db, dt, D_MODEL)

    k_p = jnp.transpose(kt_p.reshape(nb, B_HEADS, 2, B_DH, t), (0, 4, 1, 2, 3))
    return (y_p, y_s, st_p,
            k_p, v_p.reshape(nb, t, B_HEADS, B_DV),
            st_s,
            k_s.reshape(db, dt, B_HEADS, 2, B_DH), v_s.reshape(db, dt, B_HEADS, B_DV))
```

```python
import functools
import math

import jax
import jax.numpy as jnp
import numpy as np
from jax import lax
from jax.experimental import pallas as pl
from jax.experimental.pallas import tpu as pltpu

D_MODEL = 1024
DEPTH = 2
N_A_LAYERS = DEPTH // 2
PAGE_SIZE = 128
A_HEADS = 8
A_DK = 128
A_DV = D_MODEL // A_HEADS
A_FDIM = A_HEADS * A_DK
B_HEADS = 8
B_DH = D_MODEL // (2 * B_HEADS)
B_DV = 2 * B_DH
FFN_DIM = 2816
NORM_EPS = 1e-6
MASK_VALUE = -1e30

LANES = 128
BF16_SUBLANES = 16
VMEM_LIMIT_BYTES = 48 * 1024 * 1024

F32 = jnp.float32
BF16 = jnp.bfloat16


def _rms(x, g):
    ms = jnp.mean(x * x, axis=-1, keepdims=True)
    return x * lax.rsqrt(ms + NORM_EPS) * g


def _silu(x):
    return x * jax.nn.sigmoid(x)


def _dot(a, b):
    return jnp.dot(a, b, preferred_element_type=F32)


def _dot_nt(a, b):
    return lax.dot_general(a, b, (((1,), (1,)), ((), ())), preferred_element_type=F32)


def _dot_tn(a, b):
    return lax.dot_general(a, b, (((0,), (0,)), ((), ())), preferred_element_type=F32)


def _resident(shape):
    nd = len(shape)
    return pl.BlockSpec(shape, lambda *_: (0,) * nd, pipeline_mode=pl.Buffered(1))


def _row_tile(m):
    return min(m, 512)


def _params(*sem):
    return pltpu.CompilerParams(dimension_semantics=sem, vmem_limit_bytes=VMEM_LIMIT_BYTES)


FFN_CHUNK = 256


def _swiglu_body(x_ref, gpre_ref, gpost_ref, wgu_ref, wd_ref, o_ref, acc_ref, after_chunk=None):
    x = x_ref[...]
    xn = _rms(x, gpre_ref[...]).astype(BF16)
    for c in range(FFN_DIM // FFN_CHUNK):
        lo = c * FFN_CHUNK
        gate = _dot(xn, wgu_ref[:, lo:lo + FFN_CHUNK])
        up = _dot(xn, wgu_ref[:, FFN_DIM + lo:FFN_DIM + lo + FFN_CHUNK])
        act = (_silu(gate) * up).astype(BF16)
        down = _dot(act, wd_ref[lo:lo + FFN_CHUNK, :])
        if c == 0:
            acc_ref[...] = down
        else:
            acc_ref[...] += down
        if after_chunk is not None:
            after_chunk(c)
    o_ref[...] = x + 0.5 * _rms(acc_ref[...], gpost_ref[...])


def _swiglu_kernel(x_ref, gpre_ref, gpost_ref, wgu_ref, wd_ref, o_ref, acc_ref):
    _swiglu_body(x_ref, gpre_ref, gpost_ref, wgu_ref, wd_ref, o_ref, acc_ref)


def _swiglu_block(x, g_pre, g_post, w_gu, w_down, which):
    m = x.shape[0]
    tm = _row_tile(m)
    row = pl.BlockSpec((tm, D_MODEL), lambda i: (i, 0))
    vec = _resident((1, D_MODEL))

    def picked(w):
        return pl.BlockSpec((None, None) + w.shape[2:], lambda i: which + (0, 0), pipeline_mode=pl.Buffered(1))

    return pl.pallas_call(
        _swiglu_kernel,
        grid=(m // tm,),
        in_specs=[row, vec, vec, picked(w_gu), picked(w_down)],
        out_specs=row,
        out_shape=jax.ShapeDtypeStruct((m, D_MODEL), F32),
        scratch_shapes=[pltpu.VMEM((tm, D_MODEL), F32)],
        compiler_params=_params("parallel"),
        name="swiglu_block",
    )(x, g_pre.reshape(1, -1), g_post.reshape(1, -1), w_gu, w_down)


def _norm_matmul_kernel(x_ref, g_ref, w_ref, *o_refs):
    xn = _rms(x_ref[...], g_ref[...]).astype(BF16)
    off = 0
    for o_ref in o_refs:
        n = o_ref.shape[-1]
        o_ref[...] = _dot(xn, w_ref[:, off:off + n])
        off += n


def _norm_matmul(x, g, w, splits):
    m = x.shape[0]
    tm = _row_tile(m)
    return pl.pallas_call(
        _norm_matmul_kernel,
        grid=(m // tm,),
        in_specs=[pl.BlockSpec((tm, D_MODEL), lambda i: (i, 0)), _resident((1, D_MODEL)), _resident(w.shape)],
        out_specs=[pl.BlockSpec((tm, n), lambda i: (i, 0)) for n in splits],
        out_shape=[jax.ShapeDtypeStruct((m, n), F32) for n in splits],
        compiler_params=_params("parallel"),
        name="norm_matmul",
    )(x, g.reshape(1, -1), w)


def _matmul_norm_res_kernel(a_ref, w_ref, g_ref, x_ref, o_ref):
    o_ref[...] = x_ref[...] + _rms(_dot(a_ref[...], w_ref[...]), g_ref[...])


def _matmul_norm_res(a, w, g, x):
    m = x.shape[0]
    tm = _row_tile(m)
    row = pl.BlockSpec((tm, D_MODEL), lambda i: (i, 0))
    return pl.pallas_call(
        _matmul_norm_res_kernel,
        grid=(m // tm,),
        in_specs=[row, _resident(w.shape), _resident((1, D_MODEL)), row],
        out_specs=row,
        out_shape=jax.ShapeDtypeStruct((m, D_MODEL), F32),
        compiler_params=_params("parallel"),
        name="matmul_norm_res",
    )(a, w, g.reshape(1, -1), x)


SUB = 16
LOG2_E = math.log2(math.e)


def _cumsum_rows(x, tri):
    hi = x.astype(BF16)
    r1 = x - hi.astype(F32)
    mid = r1.astype(BF16)
    lo = (r1 - mid.astype(F32)).astype(BF16)
    return _dot(tri, hi) + _dot(tri, mid) + _dot(tri, lo)


HGRN_TILE = 256


def _hgrn_kernel(q_ref, f_ref, i_ref, g_ref, lb_ref, gn_ref, s0_ref, x_ref, wout_ref, gpost_ref, o_ref, s_out_ref,
                 st_ref, b_ref, b2_ref, k_ref, mix_ref, *, chunk, t_valid):
    tile = q_ref.shape[1]
    n_sub = chunk // SUB
    ti = pl.program_id(1)
    heads = [slice(h * A_DK, (h + 1) * A_DK) for h in range(A_HEADS)]
    lb = lb_ref[...]
    gn = gn_ref[...]
    ri = lax.broadcasted_iota(jnp.int32, (chunk, chunk), 0)
    ci = lax.broadcasted_iota(jnp.int32, (chunk, chunk), 1)
    tri = jnp.where(ri >= ci, 1.0, 0.0).astype(BF16)
    half = SUB // 2
    sub_col = lax.broadcasted_iota(jnp.int32, (half, SUB), 1)
    causal_sub = (lax.broadcasted_iota(jnp.int32, (SUB, SUB), 0) >= lax.broadcasted_iota(jnp.int32, (SUB, SUB), 1))

    @pl.when(ti == 0)
    def _():
        for h in range(A_HEADS):
            st_ref[h] = s0_ref[0, h].T

    def body(c, carry):
        base = pl.multiple_of(c * chunk, chunk)
        rows = pl.ds(base, chunk)
        qq = _silu(q_ref[0, rows, :])
        forget = lb + (1.0 - lb) * jax.nn.sigmoid(f_ref[0, rows, :])
        logf = jnp.log(forget)
        kk = 1.0 - forget
        if t_valid % chunk:
            live = (ti * tile + base + lax.broadcasted_iota(jnp.int32, (chunk, 1), 0)) < t_valid
            qq = jnp.where(live, qq, 0.0)
            logf = jnp.where(live, logf, 0.0)
            kk = jnp.where(live, kk, 0.0)
        b = _cumsum_rows(logf, tri)
        b_ref[...] = b
        b2_ref[...] = b * LOG2_E
        k_ref[...] = kk
        vb = i_ref[0, rows, :].astype(BF16)
        b_last = b_ref[chunk - 1:chunk, :]
        qe = (qq * jnp.exp(b)).astype(BF16)
        kd = (kk * jnp.exp(b_last - b)).astype(BF16)
        decay = jnp.exp(b_last)
        gate = _silu(g_ref[0, rows, :])

        intra = []
        for i in range(n_sub):
            r0 = i * SUB
            qi = qq[r0:r0 + SUB]
            bi = b[r0:r0 + SUB]
            if i == 0:
                acc = [jnp.zeros((SUB, A_DV), F32) for _ in heads]
            else:
                ref_row = b_ref[r0 - 1:r0, :]
                qp = (qi * jnp.exp(bi - ref_row)).astype(BF16)
                kp = (kk[:r0] * jnp.exp(ref_row - b[:r0])).astype(BF16)
                acc = [_dot(_dot_nt(qp[:, hs], kp[:, hs]).astype(BF16), vb[:r0, hs]) for hs in heads]
            top = [jnp.zeros((half, SUB), F32) for _ in heads]
            bot = [jnp.zeros((half, SUB), F32) for _ in heads]
            b2i = bi * LOG2_E
            for s in range(SUB):
                first = 0 if s < half else half
                row_s = slice(r0 + s, r0 + s + 1)
                w = qi[first:] * jnp.exp2(jnp.minimum(b2i[first:] - b2_ref[row_s, :], 0.0)) * k_ref[row_s, :]
                for h, hs in enumerate(heads):
                    a = jnp.sum(w[:, hs], axis=-1, keepdims=True)
                    if s < half:
                        top[h] = jnp.where(sub_col == s, a[:half], top[h])
                    bot[h] = jnp.where(sub_col == s, a[-half:], bot[h])
            for h, hs in enumerate(heads):
                att = jnp.where(causal_sub, jnp.concatenate([top[h], bot[h]], axis=0), 0.0)
                acc[h] = acc[h] + _dot(att.astype(BF16), vb[r0:r0 + SUB, hs])
            intra.append(acc)

        for h, hs in enumerate(heads):
            st = st_ref[h]
            o = _dot_nt(qe[:, hs], st.astype(BF16)) + jnp.concatenate([acc[h] for acc in intra], axis=0)
            st_ref[h] = st * decay[:, hs] + _dot_tn(vb[:, hs], kd[:, hs])
            mix_ref[rows, hs] = (_rms(o, gn) * gate[:, hs]).astype(BF16)
        return carry

    lax.fori_loop(0, tile // chunk, body, 0, unroll=True)
    o_ref[0] = x_ref[0] + _rms(_dot(mix_ref[...], wout_ref[...]), gpost_ref[...])

    @pl.when(ti == pl.num_programs(1) - 1)
    def _():
        for h in range(A_HEADS):
            s_out_ref[0, h] = st_ref[h].T


def _hgrn_mixer(q, f, i, g, lb, gnorm, s0, x, w_out, g_post, *, chunk, t_valid):
    nb, t_pad, _ = q.shape
    tile = min(HGRN_TILE, t_pad)
    seq = pl.BlockSpec((1, tile, D_MODEL), lambda b, t: (b, t, 0))
    state = pl.BlockSpec((1, A_HEADS, A_DK, A_DV), lambda b, t: (b, 0, 0, 0))
    kern = functools.partial(_hgrn_kernel, chunk=chunk, t_valid=t_valid)
    return pl.pallas_call(
        kern,
        grid=(nb, t_pad // tile),
        in_specs=[seq, seq, seq, seq,
                  pl.BlockSpec((1, A_FDIM), lambda b, t: (0, 0)),
                  pl.BlockSpec((1, A_DV), lambda b, t: (0, 0)),
                  state, seq, _resident(w_out.shape), _resident((1, D_MODEL))],
        out_specs=[seq, state],
        out_shape=[jax.ShapeDtypeStruct((nb, t_pad, D_MODEL), F32),
                   jax.ShapeDtypeStruct((nb, A_HEADS, A_DK, A_DV), F32)],
        scratch_shapes=[pltpu.VMEM((A_HEADS, A_DV, A_DK), F32)] + [pltpu.VMEM((chunk, A_FDIM), F32)] * 3
                       + [pltpu.VMEM((tile, D_MODEL), BF16)],
        compiler_params=_params("parallel", "arbitrary"),
        name="hgrn_mixer",
    )(q, f, i, g, lb.reshape(1, -1), gnorm.reshape(1, -1), s0, x, w_out, g_post.reshape(1, -1))


ATTN_TILE = 512
POS_RADIX = 16


def _attn_prompt_kernel(slopes_ref, lam_ref, qt_ref, k_ref, vt_ref, sub_ref, o_ref, ka_ref, vtb_ref,
                        sa_ref, sb_ref, pa_ref, pb_ref, *, out_scale):
    h = pl.program_id(1)
    t = k_ref.shape[1]
    tq = tk = min(ATTN_TILE, t)
    n_kv = t // tk

    lane = lax.broadcasted_iota(jnp.int32, (tk, 2 * B_DH), 1)
    row = lax.broadcasted_iota(jnp.int32, (tk, 2 * B_DH), 0)
    for jt in range(n_kv):
        kpos = jt * tk + row
        digits = jnp.where(lane == B_DH, kpos // POS_RADIX, jnp.where(lane == B_DH + 1, kpos % POS_RADIX,
                           jnp.where(lane < B_DH + 4, 1, 0))).astype(F32)
        kblk = k_ref[0, jt * tk:(jt + 1) * tk, :]
        ka_ref[jt, 0] = jnp.where(lane < B_DH, kblk, digits).astype(BF16)
        ka_ref[jt, 1] = jnp.where(lane < B_DH, pltpu.roll(kblk, B_DH, axis=1), digits).astype(BF16)
        vtb_ref[jt] = vt_ref[0, :, jt * tk:(jt + 1) * tk].astype(BF16)

    slope = slopes_ref[h]
    lam = lam_ref[0]
    r = lax.broadcasted_iota(jnp.int32, (B_DH, tq), 0)
    causal = (lax.broadcasted_iota(jnp.int32, (tk, tq), 0) <= lax.broadcasted_iota(jnp.int32, (tk, tq), 1))
    s_bufs, p_bufs = (sa_ref, sb_ref), (pa_ref, pb_ref)
    one = (jnp.full((1, tq), MASK_VALUE, F32), jnp.zeros((1, tq), F32))
    zero = jnp.zeros((B_DV, tq), F32)

    def query_tile(qi):
        qt = qt_ref[0, :, qi * tq:(qi + 1) * tq] * (B_DH ** -0.5)
        qpos = qi * tq + lax.broadcasted_iota(jnp.int32, (B_DH, tq), 1)
        hi = (qpos // POS_RADIX).astype(F32)
        lo = (qpos % POS_RADIX).astype(F32)
        bias_rows = jnp.where(r == 0, POS_RADIX * slope, jnp.where(r == 1, slope,
                              jnp.where(r == 2, -POS_RADIX * slope * hi, jnp.where(r == 3, -slope * lo, 0.0))))
        qa = [jnp.concatenate([qt[j * B_DH:(j + 1) * B_DH], bias_rows], axis=0).astype(BF16) for j in range(2)]

        def put_scores(buf, tile, diagonal):
            for j in range(2):
                s = _dot(ka_ref[tile, j], qa[j])
                buf[j] = jnp.where(causal, s, MASK_VALUE) if diagonal else s

        def pipeline_step(slot, stats, accs, next_tile, prev_tile):
            if next_tile is not None:
                put_scores(s_bufs[1 - slot], next_tile, False)
            new_stats, new_accs = [], []
            for j in range(2):
                m, l = stats[j]
                pv = _dot(vtb_ref[prev_tile], p_bufs[1 - slot][j])
                s = s_bufs[slot][j]
                m_new = jnp.maximum(m, jnp.max(s, axis=0, keepdims=True))
                alpha = jnp.exp(m - m_new)
                p = jnp.exp(s - m_new)
                p_bufs[slot][j] = p.astype(BF16)
                new_stats.append((m_new, alpha * l + jnp.sum(p, axis=0, keepdims=True)))
                new_accs.append(alpha * (accs[j] + pv))
            return tuple(new_stats), tuple(new_accs)

        def final_step(slot, stats, accs, prev_tile):
            stats, accs = pipeline_step(slot, stats, accs, None, prev_tile)
            last_tile = max(qi - 1, 0)
            o1, o2 = [(accs[j] + _dot(vtb_ref[last_tile], p_bufs[slot][j])) / stats[j][1] for j in range(2)]
            ot = o1 - lam * o2
            ms = jnp.mean(ot * ot, axis=0, keepdims=True)
            ot = ot * lax.rsqrt(ms + NORM_EPS) * (sub_ref[...] * out_scale)
            o_ref[0, qi * tq:(qi + 1) * tq, :] = ot.T.astype(BF16)

        put_scores(sa_ref, qi, True)
        pb_ref[...] = jnp.zeros(pb_ref.shape, BF16)
        stats, accs = (one, one), (zero, zero)
        for k in range(0, qi - 1, 2):
            stats, accs = pipeline_step(0, stats, accs, next_tile=k, prev_tile=max(k - 2, 0))
            stats, accs = pipeline_step(1, stats, accs, next_tile=k + 1, prev_tile=qi if k == 0 else k - 1)
        if qi % 2 == 0:
            final_step(0, stats, accs, max(qi - 2, 0))
        else:
            k = qi - 1
            stats, accs = pipeline_step(0, stats, accs, next_tile=k, prev_tile=max(k - 2, 0))
            final_step(1, stats, accs, qi if k == 0 else k - 1)

    for qi in range(n_kv):
        query_tile(qi)


def _attn_prompt(qt, k, vt, slopes, lam, subln, out_scale):
    nb, t, _ = k.shape
    tq = min(ATTN_TILE, t)
    smem = pl.BlockSpec(memory_space=pltpu.SMEM)
    kern = functools.partial(_attn_prompt_kernel, out_scale=out_scale)
    return pl.pallas_call(
        kern,
        grid=(nb, B_HEADS),
        in_specs=[smem, smem,
                  pl.BlockSpec((1, 2 * B_DH, t), lambda b, h: (b, h, 0)),
                  pl.BlockSpec((1, t, 2 * B_DH), lambda b, h: (b, 0, h)),
                  pl.BlockSpec((1, B_DV, t), lambda b, h: (b, h, 0)),
                  pl.BlockSpec((B_DV, 1), lambda b, h: (0, 0))],
        out_specs=pl.BlockSpec((1, t, B_DV), lambda b, h: (b, 0, h)),
        out_shape=jax.ShapeDtypeStruct((nb, t, D_MODEL), BF16),
        scratch_shapes=[pltpu.VMEM((t // tq, 2, tq, 2 * B_DH), BF16), pltpu.VMEM((t // tq, B_DV, tq), BF16),
                        pltpu.VMEM((2, tq, tq), F32), pltpu.VMEM((2, tq, tq), F32),
                        pltpu.VMEM((2, tq, tq), BF16), pltpu.VMEM((2, tq, tq), BF16)],
        compiler_params=_params("parallel", "parallel"),
        name="attn_prompt",
    )(slopes, lam.reshape(1), qt, k, vt, subln.reshape(-1, 1))


N_MAPS = 2 * B_HEADS
PAGE_COLS = PAGE_SIZE * B_HEADS
PAGE_GROUP = 4
HOST_CALLS = 4
PAGE_LOOKAHEAD = 3
K_SLOTS = PAGE_LOOKAHEAD + 1
V_SLOTS = PAGE_LOOKAHEAD + 3
HOSTING_VMEM_LIMIT_BYTES = 56 * 1024 * 1024


def _block_diag_q(q_row):
    col = lax.broadcasted_iota(jnp.int32, (N_MAPS, D_MODEL), 1)
    row = lax.broadcasted_iota(jnp.int32, (N_MAPS, D_MODEL), 0)
    mine = (col // B_DV == row % B_HEADS) & ((col // B_DH) % 2 == row // B_HEADS)
    return jnp.where(mine, q_row * (B_DH ** -0.5), 0.0)


def _swiglu_pages_kernel(pt_ref, x_ref, gpre_ref, gpost_ref, wgu_ref, wd_ref, q_ref, slope_ref, spread_ref,
                         m_in_ref, l_in_ref, acc_in_ref, ck_ref, cv_ref, o_ref, m_ref, l_ref, acc_ref,
                         ffn_acc_ref, kbuf_ref, vbuf_ref, ksem_ref, vsem_ref, s_ref, e_ref, pm_ref, mg_ref,
                         *, first_page, n_groups, n_pages):
    i = pl.program_id(0)
    n_ahead = min(PAGE_LOOKAHEAD, n_groups)

    def copies(g, step=None):
        out = []
        for j in range(PAGE_GROUP):
            page = pt_ref[i if step is None else step, first_page + g * PAGE_GROUP + j]
            ks, vs = g % K_SLOTS, g % V_SLOTS
            out.append(pltpu.make_async_copy(ck_ref.at[page], kbuf_ref.at[ks, j], ksem_ref.at[ks, j]))
            out.append(pltpu.make_async_copy(cv_ref.at[page], vbuf_ref.at[vs, j], vsem_ref.at[vs, j]))
        return out

    def slots_free_after_chunk(g):
        last_v = max([u for u in range(n_groups) if u % V_SLOTS == g % V_SLOTS], default=-3)
        last_k = max([u for u in range(n_groups) if u % K_SLOTS == g % K_SLOTS], default=0)
        return max(last_v + 3, last_k, 0)

    @pl.when(i == 0)
    def _():
        for g in range(n_ahead):
            for cp in copies(g):
                cp.start()

    m_ref[...] = m_in_ref[...]
    l_ref[...] = l_in_ref[...]
    acc_ref[...] = acc_in_ref[...]
    qbd = _block_diag_q(q_ref[0]).astype(BF16)
    row_head = lax.broadcasted_iota(jnp.int32, (N_MAPS, 1), 0) % B_HEADS
    own_head = lax.broadcasted_iota(jnp.int32, (N_MAPS, PAGE_COLS), 1) % B_HEADS == row_head
    lane_pos = lax.broadcasted_iota(jnp.int32, (N_MAPS, PAGE_GROUP * PAGE_SIZE), 1)

    def scores(g):
        for cp in copies(g):
            cp.wait()
        if g + PAGE_LOOKAHEAD < n_groups:
            for cp in copies(g + PAGE_LOOKAHEAD):
                cp.start()
        keys = jnp.concatenate([kbuf_ref[g % K_SLOTS, j].astype(BF16) for j in range(PAGE_GROUP)], axis=1)
        s = _dot(qbd, keys)
        kpos = (first_page + g * PAGE_GROUP) * PAGE_SIZE + lane_pos
        s_ref[g % 2] = s - slope_ref[...] * (n_pages * PAGE_SIZE - kpos).astype(F32)

    def softmax_step(g):
        s = s_ref[g % 2]
        m = m_ref[0]
        m_new = jnp.maximum(m, jnp.max(s, axis=-1, keepdims=True))
        alpha = jnp.exp(m - m_new)
        e = jnp.exp(s - m_new[:, :1])
        e_ref[g % 2] = e.astype(BF16)
        l_ref[0] = alpha * l_ref[0] + jnp.sum(e, axis=-1, keepdims=True)
        acc_ref[0] = alpha * acc_ref[0]
        m_ref[0] = m_new
        mg_ref[g % V_SLOTS] = m_new

    def spread_probs(g):
        e = e_ref[g % 2]
        stacked = jnp.concatenate([e[:, j * PAGE_SIZE:(j + 1) * PAGE_SIZE] for j in range(PAGE_GROUP)], axis=0)
        spread = _dot(stacked, spread_ref[...])
        for j in range(PAGE_GROUP):
            pm_ref[g % 2, :, j * PAGE_COLS:(j + 1) * PAGE_COLS] = jnp.where(
                own_head, spread[j * N_MAPS:(j + 1) * N_MAPS], 0.0).astype(BF16)

    def add_values(g):
        values = vbuf_ref[g % V_SLOTS].reshape(PAGE_GROUP * PAGE_COLS, B_DV).astype(BF16)
        pv = _dot(pm_ref[g % 2], values)
        acc_ref[0] = acc_ref[0] + jnp.exp(mg_ref[g % V_SLOTS] - m_ref[0]) * pv

    def after_chunk(c):
        for stage, g in ((add_values, c - 3), (spread_probs, c - 2), (softmax_step, c - 1), (scores, c)):
            if 0 <= g < n_groups:
                stage(g)
        for g in range(n_ahead):
            if slots_free_after_chunk(g) == c:
                @pl.when(i + 1 < pl.num_programs(0))
                def _():
                    for cp in copies(g, i + 1):
                        cp.start()

    assert all(slots_free_after_chunk(g) < FFN_DIM // FFN_CHUNK for g in range(n_ahead))
    _swiglu_body(x_ref, gpre_ref, gpost_ref, wgu_ref, wd_ref, o_ref, ffn_acc_ref, after_chunk)


def _swiglu_block_hosting_pages(x, g_pre, g_post, w_gu, w_down, which, q, slope_rows, state, ck, cv,
                                page_table, call_index):
    m_rows = x.shape[0]
    tm = _row_tile(m_rows)
    nb, n_pages = page_table.shape
    per_call = n_pages // HOST_CALLS
    n_groups = per_call // PAGE_GROUP
    assert m_rows // tm == nb and per_call * HOST_CALLS == n_pages and n_groups * PAGE_GROUP == per_call
    assert n_groups + 3 <= FFN_DIM // FFN_CHUNK
    tok = lax.broadcasted_iota(jnp.int32, (PAGE_SIZE, PAGE_COLS), 0)
    dst = lax.broadcasted_iota(jnp.int32, (PAGE_SIZE, PAGE_COLS), 1)
    spread = jnp.where(dst // B_HEADS == tok, 1.0, 0.0).astype(BF16)
    row = pl.BlockSpec((tm, D_MODEL), lambda i, pt: (i, 0))
    vec = _resident((1, D_MODEL))
    st = pl.BlockSpec((1, N_MAPS, B_DV), lambda i, pt: (i, 0, 0))
    any_space = pl.BlockSpec(memory_space=pl.ANY)

    def picked(w):
        return pl.BlockSpec((None, None) + w.shape[2:], lambda *_: which + (0, 0), pipeline_mode=pl.Buffered(1))

    kern = functools.partial(_swiglu_pages_kernel, first_page=call_index * per_call, n_groups=n_groups,
                             n_pages=n_pages)
    st_shape = jax.ShapeDtypeStruct((nb, N_MAPS, B_DV), F32)
    out = pl.pallas_call(
        kern,
        grid_spec=pltpu.PrefetchScalarGridSpec(
            num_scalar_prefetch=1,
            grid=(nb,),
            in_specs=[row, vec, vec, picked(w_gu), picked(w_down),
                      pl.BlockSpec((1, 1, D_MODEL), lambda i, pt: (i, 0, 0)),
                      _resident((N_MAPS, 1)), _resident(spread.shape), st, st, st, any_space, any_space],
            out_specs=[row, st, st, st],
            scratch_shapes=[pltpu.VMEM((tm, D_MODEL), F32),
                            pltpu.VMEM((K_SLOTS, PAGE_GROUP, D_MODEL, PAGE_SIZE), F32),
                            pltpu.VMEM((V_SLOTS, PAGE_GROUP, PAGE_COLS, B_DV), F32),
                            pltpu.SemaphoreType.DMA((K_SLOTS, PAGE_GROUP)),
                            pltpu.SemaphoreType.DMA((V_SLOTS, PAGE_GROUP)),
                            pltpu.VMEM((2, N_MAPS, PAGE_GROUP * PAGE_SIZE), F32),
                            pltpu.VMEM((2, N_MAPS, PAGE_GROUP * PAGE_SIZE), BF16),
                            pltpu.VMEM((2, N_MAPS, PAGE_GROUP * PAGE_COLS), BF16),
                            pltpu.VMEM((V_SLOTS, N_MAPS, B_DV), F32)]),
        out_shape=[jax.ShapeDtypeStruct((m_rows, D_MODEL), F32), st_shape, st_shape, st_shape],
        compiler_params=pltpu.CompilerParams(dimension_semantics=("arbitrary",),
                                             vmem_limit_bytes=HOSTING_VMEM_LIMIT_BYTES),
        name="swiglu_block_hosting_pages",
    )(page_table, x, g_pre.reshape(1, -1), g_post.reshape(1, -1), w_gu, w_down, q, slope_rows, spread,
      *state, ck, cv)
    return out[0], tuple(out[1:])


def _attn_finish_kernel(lam_ref, q_ref, kn_ref, vn_ref, sub_ref, m_ref, l_ref, acc_ref, o_ref, *, out_scale):
    s_new = jnp.sum(_block_diag_q(q_ref[0]) * kn_ref[0], axis=-1, keepdims=True)
    m = m_ref[0][:, :1]
    m_new = jnp.maximum(m, s_new)
    alpha = jnp.exp(m - m_new)
    e = jnp.exp(s_new - m_new)
    vn = vn_ref[0]
    l = alpha * l_ref[0] + e
    acc = alpha * acc_ref[0] + e * jnp.concatenate([vn, vn], axis=0)
    o = acc / l
    o = o[:B_HEADS] - lam_ref[0] * o[B_HEADS:]
    o_ref[0] = _rms(o, sub_ref[...]) * out_scale


def _attn_sample_finish(q, k_new, v_new, state, lam, subln, out_scale):
    nb = q.shape[0]
    row = pl.BlockSpec((1, 1, D_MODEL), lambda b: (b, 0, 0))
    per_head = pl.BlockSpec((1, B_HEADS, B_DV), lambda b: (b, 0, 0))
    st = pl.BlockSpec((1, N_MAPS, B_DV), lambda b: (b, 0, 0))
    out = pl.pallas_call(
        functools.partial(_attn_finish_kernel, out_scale=out_scale),
        grid=(nb,),
        in_specs=[pl.BlockSpec(memory_space=pltpu.SMEM), row, row, per_head,
                  pl.BlockSpec((1, B_DV), lambda b: (0, 0)), st, st, st],
        out_specs=per_head,
        out_shape=jax.ShapeDtypeStruct((nb, B_HEADS, B_DV), F32),
        compiler_params=_params("parallel"),
        name="attn_sample_finish",
    )(lam.reshape(1), q, k_new, v_new.reshape(nb, B_HEADS, B_DV), subln.reshape(1, -1), *state)
    return out.reshape(nb, D_MODEL)


def _diff_lambda(lam, layer_idx):
    lam_init = 0.8 - 0.6 * math.exp(-0.3 * layer_idx)
    lam = lam.astype(F32)
    val = jnp.exp(jnp.sum(lam[0] * lam[1])) - jnp.exp(jnp.sum(lam[2] * lam[3])) + lam_init
    return val, lam_init


def _norm_proj_kernel(x_ref, g_ref, *refs, layouts):
    n_w = len(layouts)
    w_refs, o_refs = refs[:n_w], list(refs[n_w:])
    xn = _rms(x_ref[...], g_ref[...]).astype(BF16)
    for w_ref, layout in zip(w_refs, layouts):
        y = _dot(xn, w_ref[...])
        if "n" in layout:
            o_refs.pop(0)[...] = y
        if "t" in layout:
            o_refs.pop(0)[0] = y.T


def _norm_proj(x, g, weights, layouts, nb, t):
    m = x.shape[0]
    tm = _row_tile(t)
    per_seq = t // tm
    nat = (pl.BlockSpec((tm, D_MODEL), lambda i: (i, 0)), jax.ShapeDtypeStruct((m, D_MODEL), F32))
    tra = (pl.BlockSpec((1, D_MODEL, tm), lambda i: (i // per_seq, 0, i % per_seq)),
           jax.ShapeDtypeStruct((nb, D_MODEL, t), F32))
    outs = [{"n": nat, "t": tra}[c] for layout in layouts for c in layout]
    return pl.pallas_call(
        functools.partial(_norm_proj_kernel, layouts=layouts),
        grid=(m // tm,),
        in_specs=[pl.BlockSpec((tm, D_MODEL), lambda i: (i, 0)), _resident((1, D_MODEL))]
                 + [_resident(w.shape) for w in weights],
        out_specs=[o[0] for o in outs],
        out_shape=[o[1] for o in outs],
        compiler_params=_params("parallel"),
        name="norm_proj",
    )(x, g.reshape(1, -1), *weights)


def _alibi_slopes():
    slopes = [2.0 ** (-8.0 * i / B_HEADS) for i in range(1, B_HEADS + 1)]
    assert all(float(np.asarray(s, dtype=BF16)) == s for s in slopes)
    return jnp.asarray(slopes, F32)


ATTN_LAYER = N_A_LAYERS
assert DEPTH == 2


def _trunk_head(x, hgrn_state, project_kv, swiglu, w, *, chunk):
    nb, t, _ = x.shape
    m = nb * t
    x = x.reshape(m, D_MODEL)
    lbs = jnp.cumsum(jax.nn.softmax(w["a_lb"].astype(F32), axis=0), axis=0)
    t_pad = -(-t // chunk) * chunk
    g = w["norm_g"][0]
    x = swiglu(x, g[0], g[1], (0, 0))
    proj = _norm_matmul(x, g[2], w["a_w_in"][0], (A_FDIM, A_FDIM, D_MODEL, D_MODEL))
    seqs = [a.reshape(nb, t, D_MODEL) for a in list(proj) + [x]]
    if t_pad != t:
        seqs = [jnp.pad(a, ((0, 0), (0, t_pad - t), (0, 0))) for a in seqs]
    x, s_new = _hgrn_mixer(*seqs[:4], lbs[0], w["a_gnorm"][0], hgrn_state[0], seqs[4], w["a_w_out"][0], g[3],
                           chunk=chunk, t_valid=t)
    x = x[:, :t].reshape(m, D_MODEL)
    x = swiglu(x, g[4], g[5], (0, 1))
    kv = project_kv(x)
    g = w["norm_g"][ATTN_LAYER]
    x = swiglu(x, g[0], g[1], (ATTN_LAYER, 0))
    return x, s_new[None], kv


def _trunk_tail(x, attn_out, swiglu, w):
    g = w["norm_g"][ATTN_LAYER]
    x = _matmul_norm_res(attn_out.astype(BF16), w["b_w_o"][0], g[3], x)
    return swiglu(x, g[4], g[5], (ATTN_LAYER, 1))


def kernel(x_prompt, x_sample, cache_k, cache_v, state_hgrn, page_table, norm_g, w_ffn_gu, w_ffn_down,
           a_w_in, a_w_out, a_lb, a_gnorm, norm_kv, w_kv, b_w_q, b_w_o, b_lambda, b_subln):
    w = dict(norm_g=norm_g, a_lb=a_lb, a_gnorm=a_gnorm,
             w_ffn_gu=w_ffn_gu.astype(BF16), w_ffn_down=w_ffn_down.astype(BF16),
             a_w_in=a_w_in.astype(BF16), a_w_out=a_w_out.astype(BF16), b_w_o=b_w_o.astype(BF16))
    w_kv, w_q = w_kv.astype(BF16), b_w_q.astype(BF16)[0]
    nb, t, _ = x_prompt.shape
    db, dt, _ = x_sample.shape
    n_phys = cache_k.shape[0]
    slopes = _alibi_slopes()
    lam, lam_init = _diff_lambda(b_lambda[0], ATTN_LAYER)
    out_scale = 1.0 - lam_init
    g_q = norm_g[ATTN_LAYER, 2]

    def plain_swiglu(x, g_pre, g_post, which):
        return _swiglu_block(x, g_pre, g_post, w["w_ffn_gu"], w["w_ffn_down"], which)

    x_s, st_s, (k_s, v_s) = _trunk_head(
        x_sample, state_hgrn, lambda x: _norm_matmul(x, norm_kv, w_kv, (D_MODEL, D_MODEL)), plain_swiglu, w,
        chunk=BF16_SUBLANES)
    (q_s,) = _norm_matmul(x_s, g_q, w_q, (D_MODEL,))
    as_tok = lambda a: a.reshape(db, 1, D_MODEL)

    ck = jnp.transpose(cache_k, (0, 2, 3, 4, 1)).reshape(n_phys, D_MODEL, PAGE_SIZE)
    cv = cache_v.reshape(n_phys, PAGE_COLS, B_DV)
    slope_rows = jnp.tile(slopes, 2).reshape(N_MAPS, 1)
    cache_state = [(jnp.full((db, N_MAPS, B_DV), MASK_VALUE, F32),) + (jnp.zeros((db, N_MAPS, B_DV), F32),) * 2]

    def hosting_swiglu(x, g_pre, g_post, which):
        call_index = len(cache_state) - 1
        y, state = _swiglu_block_hosting_pages(x, g_pre, g_post, w["w_ffn_gu"], w["w_ffn_down"], which,
                                               as_tok(q_s), slope_rows, cache_state[-1], ck, cv, page_table,
                                               call_index)
        cache_state.append(state)
        return y

    s0_prompt = jnp.zeros((N_A_LAYERS, nb, A_HEADS, A_DK, A_DV), state_hgrn.dtype)
    x_p, st_p, (k_nat, kt_p, v_p, vt_p) = _trunk_head(
        x_prompt, s0_prompt,
        lambda x: _norm_proj(x, norm_kv, [w_kv[:, :D_MODEL], w_kv[:, D_MODEL:]], ["nt", "nt"], nb, t),
        hosting_swiglu, w, chunk=32)
    (qt_p,) = _norm_proj(x_p, g_q, [w_q], ["t"], nb, t)
    o_p = _attn_prompt(qt_p, k_nat.reshape(nb, t, D_MODEL), vt_p, slopes, lam, b_subln[0], out_scale)
    y_p = _trunk_tail(x_p, o_p.reshape(nb * t, D_MODEL), hosting_swiglu, w).reshape(nb, t, D_MODEL)
    assert len(cache_state) == HOST_CALLS + 1

    o_s = _attn_sample_finish(as_tok(q_s), as_tok(k_s), as_tok(v_s), cache_state[-1], lam, b_subln[0], out_scale)
    y_s = _trunk_tail(x_s, o_s, plain_swiglu, w).reshape(db, dt, D_MODEL)

    k_p = jnp.transpose(kt_p.reshape(nb, B_HEADS, 2, B_DH, t), (0, 4, 1, 2, 3))
    return (y_p, y_s, st_p,
            k_p, v_p.reshape(nb, t, B_HEADS, B_DV),
            st_s,
            k_s.reshape(db, dt, B_HEADS, 2, B_DH), v_s.reshape(db, dt, B_HEADS, B_DV))
```
